```python
import jax, jax.numpy as jnp
from jax import lax
import numpy as np

D_MODEL = 2048
BATCH = 8
SEQ = 4096
DEPTH = 2
DEC_BATCH = 16
DEC_SEQ = 32
PAST_LEN = 1024

CHUNK = 64
POOL_DIM = D_MODEL // 4
POOL_WINDOWS = (2, 4, 8, 16)
POOL_GROUPS = len(POOL_WINDOWS)
POOL_GROUP_DIM = POOL_DIM // POOL_GROUPS
POOL_HIST = max(POOL_WINDOWS) - 1
SGU_DIM = D_MODEL // 2
SGU_GROUPS = 8
SGU_GROUP_DIM = SGU_DIM // SGU_GROUPS
SGU_CHUNK = 128
CONV_DIM = D_MODEL // 4
CONV_WIDTH = 3
D_FF = -(-8 * D_MODEL // (3 * 256)) * 256
PLE_DIM = 256
N_BRANCH = 3
ALPHA = (2 * DEPTH) ** 0.25
BETA = (8 * DEPTH) ** -0.25
LN_EPS = 1e-5
OFF_U = POOL_DIM
OFF_V = OFF_U + SGU_DIM
OFF_CB = OFF_V + SGU_DIM
OFF_CC = OFF_CB + CONV_DIM
OFF_CX = OFF_CC + CONV_DIM
OFF_G = OFF_CX + CONV_DIM
IN_COLS = OFF_G + N_BRANCH * D_MODEL

kernel_name = 'hybrid_pool_sgu_conv_stream_step'


def layer_norm(x, g, b):
    xf = x.astype(jnp.float32)
    mu = jnp.mean(xf, axis=-1, keepdims=True)
    var = jnp.mean(jnp.square(xf - mu), axis=-1, keepdims=True)
    out = (xf - mu) * lax.rsqrt(var + LN_EPS) * g.astype(jnp.float32) + b.astype(jnp.float32)
    return out.astype(x.dtype)


def pool_mix(a, hist, pos0, pool_w, pool_scale):
    bsz, L, _ = a.shape
    xp = jnp.concatenate([hist, a], axis=1).astype(jnp.float32)
    cs = jnp.concatenate([jnp.zeros_like(xp[:, :1]), jnp.cumsum(xp, axis=1)], axis=1)
    end = cs[:, POOL_HIST + 1:]
    pos = pos0 + jnp.arange(L, dtype=jnp.int32) + 1
    outs = []
    for g, win in enumerate(POOL_WINDOWS):
        sl = slice(g * POOL_GROUP_DIM, (g + 1) * POOL_GROUP_DIM)
        start = cs[:, POOL_HIST + 1 - win:POOL_HIST + 1 - win + L, sl]
        cnt = jnp.minimum(pos, win).astype(jnp.float32)[None, :, None]
        outs.append((end[..., sl] - start) / cnt)
    pooled = (jnp.concatenate(outs, axis=-1) - a.astype(jnp.float32)).astype(a.dtype)
    pooled = pooled.reshape(bsz, L, POOL_GROUPS, POOL_GROUP_DIM)
    y = jnp.einsum('blgc,gcd->blgd', pooled, pool_w).reshape(bsz, L, POOL_DIM)
    return y * pool_scale


def spatial_gate(u, v, sgu_w, sgu_b):
    bsz, L, _ = v.shape
    lc = min(L, SGU_CHUNK)
    n = L // lc
    vr = v.reshape(bsz, n, lc, SGU_GROUPS, SGU_GROUP_DIM)
    w = jnp.tril(sgu_w[:, :lc, :lc])
    s = jnp.einsum('gts,bnsgc->bntgc', w, vr) + jnp.transpose(sgu_b[:, :lc])[None, None, :, :, None]
    return u * s.reshape(bsz, L, SGU_DIM)


def causal_conv(z, hist, conv_w):
    L = z.shape[1]
    zp = jnp.concatenate([hist, z], axis=1)
    y = zp[:, 0:L] * conv_w[0]
    for k in range(1, CONV_WIDTH):
        y = y + zp[:, k:k + L] * conv_w[k]
    return y


def trunk_layer(x, p, hist_pool, hist_conv, pos0, w_in, pool_w, pool_scale, sgu_ln_g, sgu_ln_b,
                sgu_w, sgu_b, conv_w, w_br_a, w_br_b, w_br_c, w_o, ln1_g, ln1_b, w_gu, w_down,
                w_pe, w_pe_gate, ln2_g, ln2_b):
    z = x @ w_in
    a_in = z[..., :OFF_U]
    u = jax.nn.gelu(z[..., OFF_U:OFF_V], approximate=False)
    v = layer_norm(jax.nn.gelu(z[..., OFF_V:OFF_CB], approximate=False), sgu_ln_g, sgu_ln_b)
    c_b = z[..., OFF_CB:OFF_CC]
    c_c = z[..., OFF_CC:OFF_CX]
    c_x = z[..., OFF_CX:OFF_G]
    g_a = jax.nn.sigmoid(z[..., OFF_G:OFF_G + D_MODEL])
    g_b = jax.nn.sigmoid(z[..., OFF_G + D_MODEL:OFF_G + 2 * D_MODEL])
    g_c = jax.nn.sigmoid(z[..., OFF_G + 2 * D_MODEL:])

    y_a = pool_mix(a_in, hist_pool, pos0, pool_w, pool_scale)
    y_b = spatial_gate(u, v, sgu_w, sgu_b)
    conv_in = c_c * c_x
    y_c = c_b * causal_conv(conv_in, hist_conv, conv_w)

    merged = g_a * (y_a @ w_br_a) + g_b * (y_b @ w_br_b) + g_c * (y_c @ w_br_c)
    x = layer_norm(ALPHA * x + merged @ w_o, ln1_g, ln1_b)

    h = x @ w_gu
    ffn = (jax.nn.silu(h[..., :D_FF]) * h[..., D_FF:]) @ w_down
    ple = jax.nn.sigmoid(x @ w_pe_gate) * (p @ w_pe)
    x = layer_norm(ALPHA * x + ffn + ple, ln2_g, ln2_b)

    new_pool = jnp.concatenate([hist_pool, a_in], axis=1)[:, -POOL_HIST:]
    new_conv = jnp.concatenate([hist_conv, conv_in], axis=1)[:, -(CONV_WIDTH - 1):]
    return x, new_pool, new_conv, v


def setup_inputs(seed: int = 0) -> dict:
    key = jax.random.key(seed)
    ks = jax.random.split(key, 32)

    def nrm(k, shape, scale=1.0):
        return jax.random.normal(k, shape, jnp.float32) * scale

    D = D_MODEL
    return {
        'x_prompt': nrm(ks[0], (BATCH, SEQ, D)),
        'x_sample': nrm(ks[1], (DEC_BATCH, DEC_SEQ, D)),
        'state_pool': nrm(ks[2], (DEPTH, DEC_BATCH, POOL_HIST, POOL_DIM)),
        'state_conv': nrm(ks[3], (DEPTH, DEC_BATCH, CONV_WIDTH - 1, CONV_DIM)),
        'p_prompt': nrm(ks[4], (DEPTH, BATCH, SEQ, PLE_DIM)),
        'p_sample': nrm(ks[5], (DEPTH, DEC_BATCH, DEC_SEQ, PLE_DIM)),
        'w_in': nrm(ks[6], (DEPTH, D, IN_COLS), D ** -0.5),
        'pool_w': nrm(ks[7], (DEPTH, POOL_GROUPS, POOL_GROUP_DIM, POOL_GROUP_DIM), POOL_GROUP_DIM ** -0.5),
        'pool_scale': 1.0 + nrm(ks[8], (DEPTH, POOL_DIM), 0.1),
        'sgu_ln_g': 1.0 + nrm(ks[9], (DEPTH, SGU_DIM), 0.01),
        'sgu_ln_b': nrm(ks[10], (DEPTH, SGU_DIM), 0.01),
        'sgu_w': nrm(ks[11], (DEPTH, SGU_GROUPS, SGU_CHUNK, SGU_CHUNK), SGU_CHUNK ** -0.5),
        'sgu_b': 1.0 + nrm(ks[12], (DEPTH, SGU_GROUPS, SGU_CHUNK), 0.1),
        'conv_w': nrm(ks[13], (DEPTH, CONV_WIDTH, CONV_DIM), CONV_WIDTH ** -0.5),
        'w_br_a': nrm(ks[14], (DEPTH, POOL_DIM, D), POOL_DIM ** -0.5),
        'w_br_b': nrm(ks[15], (DEPTH, SGU_DIM, D), SGU_DIM ** -0.5),
        'w_br_c': nrm(ks[16], (DEPTH, CONV_DIM, D), CONV_DIM ** -0.5),
        'w_o': nrm(ks[17], (DEPTH, D, D), BETA * D ** -0.5),
        'ln1_g': 1.0 + nrm(ks[18], (DEPTH, D), 0.01),
        'ln1_b': nrm(ks[19], (DEPTH, D), 0.01),
        'w_gu': nrm(ks[20], (DEPTH, D, 2 * D_FF), D ** -0.5),
        'w_down': nrm(ks[21], (DEPTH, D_FF, D), BETA * D_FF ** -0.5),
        'w_pe': nrm(ks[22], (DEPTH, PLE_DIM, D), BETA * PLE_DIM ** -0.5),
        'w_pe_gate': nrm(ks[23], (DEPTH, D, D), D ** -0.5),
        'ln2_g': 1.0 + nrm(ks[24], (DEPTH, D), 0.01),
        'ln2_b': nrm(ks[25], (DEPTH, D), 0.01),
    }


def reference(x_prompt, x_sample, state_pool, state_conv, p_prompt, p_sample, w_in, pool_w,
              pool_scale, sgu_ln_g, sgu_ln_b, sgu_w, sgu_b, conv_w, w_br_a, w_br_b, w_br_c, w_o,
              ln1_g, ln1_b, w_gu, w_down, w_pe, w_pe_gate, ln2_g, ln2_b):
    yp = x_prompt
    ys = x_sample
    zero_pool = jnp.zeros((x_prompt.shape[0], POOL_HIST, POOL_DIM), x_prompt.dtype)
    zero_conv = jnp.zeros((x_prompt.shape[0], CONV_WIDTH - 1, CONV_DIM), x_prompt.dtype)
    pool_p, conv_p, pool_s, conv_s, sgu_v_s = [], [], [], [], []
    for i in range(DEPTH):
        lw = (w_in[i], pool_w[i], pool_scale[i], sgu_ln_g[i], sgu_ln_b[i], sgu_w[i], sgu_b[i],
              conv_w[i], w_br_a[i], w_br_b[i], w_br_c[i], w_o[i], ln1_g[i], ln1_b[i], w_gu[i],
              w_down[i], w_pe[i], w_pe_gate[i], ln2_g[i], ln2_b[i])
        yp, np_pool, np_conv, _ = trunk_layer(yp, p_prompt[i], zero_pool, zero_conv, 0, *lw)
        ys, ns_pool, ns_conv, ns_v = trunk_layer(ys, p_sample[i], state_pool[i], state_conv[i],
                                                 PAST_LEN, *lw)
        pool_p.append(np_pool)
        conv_p.append(np_conv)
        pool_s.append(ns_pool)
        conv_s.append(ns_conv)
        sgu_v_s.append(ns_v)
    return (yp, ys, jnp.stack(pool_p), jnp.stack(conv_p), jnp.stack(pool_s), jnp.stack(conv_s),
            jnp.stack(sgu_v_s))
```

```python
import functools

import jax
import jax.numpy as jnp
from jax import lax
from jax.experimental import pallas as pl
from jax.experimental.pallas import tpu as pltpu

F32 = jnp.float32
BF16 = jnp.bfloat16

D_MODEL = 2048
DEPTH = 2
CHUNK_POS0_SAMPLE = 1024
POOL_DIM = 512
POOL_WINDOWS = (2, 4, 8, 16)
POOL_GROUP_DIM = 128
POOL_HIST = 15
POOL_PAD = 16
SGU_DIM = 1024
SGU_GROUPS = 8
SGU_GROUP_DIM = 128
SGU_CHUNK = 128
CONV_DIM = 512
CONV_WIDTH = 3
CONV_PAD = 8
D_FF = 5632
PLE_DIM = 256
ALPHA = (2 * DEPTH) ** 0.25
LN_EPS = 1e-5
IN_COLS = 10240
GATE_COLS = 3 * D_MODEL

TM_IN = 512
TN_IN = 1024
MC_IN = 256
TM_MIX = 256
TM_FFN = 512
TF_FFN = 512
VMEM_LIMIT = 56 * 1024 * 1024


def _layer_norm(x, g, b):
    mu = jnp.mean(x, axis=-1, keepdims=True)
    xc = x - mu
    var = jnp.mean(xc * xc, axis=-1, keepdims=True)
    return xc * lax.rsqrt(var + LN_EPS) * g + b


def _gelu(x):
    return 0.5 * x * (1.0 + lax.erf(x * (2.0 ** -0.5)))


def _const_spec(shape):
    nd = len(shape)
    return pl.BlockSpec(shape, lambda *_: (0,) * nd, pipeline_mode=pl.Buffered(1))


def _inproj_body(x_ref, w_ref, lng_ref, lnb_ref,
                 a_ref, cb_ref, conv_ref, v_ref, u_ref, g_ref, xb_ref):
    j = pl.program_id(1)

    @pl.when(j == 0)
    def _():
        xb_ref[...] = x_ref[...].astype(BF16)

    def z(m):
        return jnp.dot(xb_ref[pl.ds(m, MC_IN), :], w_ref[...], preferred_element_type=F32)

    chunks = range(0, TM_IN, MC_IN)

    @pl.when(j == 0)
    def _():
        for m in chunks:
            u_ref[pl.ds(m, MC_IN), :] = _gelu(z(m)).astype(BF16)

    @pl.when(j == 1)
    def _():
        for m in chunks:
            v_ref[pl.ds(m, MC_IN), :] = _layer_norm(_gelu(z(m)), lng_ref[...], lnb_ref[...])

    @pl.when(j == 2)
    def _():
        for m in chunks:
            zz = z(m)
            a_ref[pl.ds(m, MC_IN), :] = zz[:, :POOL_DIM]
            cb_ref[pl.ds(m, MC_IN), :] = zz[:, POOL_DIM:].astype(BF16)

    @pl.when(j == 3)
    def _():
        for m in chunks:
            zz = z(m)
            conv_ref[pl.ds(m, MC_IN), :] = zz[:, :CONV_DIM] * zz[:, CONV_DIM:]

    @pl.when(j >= 4)
    def _():
        for m in chunks:
            g_ref[pl.ds(m, MC_IN), :] = jax.nn.sigmoid(z(m)).astype(BF16)


def _inproj(x, w_in_p, ln_g, ln_b):
    n = x.shape[0]
    nj = IN_COLS // TN_IN
    row = lambda i, j: (i, 0)
    out_shape = (
        jax.ShapeDtypeStruct((n, POOL_DIM), F32),
        jax.ShapeDtypeStruct((n, CONV_DIM), BF16),
        jax.ShapeDtypeStruct((n, CONV_DIM), F32),
        jax.ShapeDtypeStruct((n, SGU_DIM), F32),
        jax.ShapeDtypeStruct((n, SGU_DIM), BF16),
        jax.ShapeDtypeStruct((n, GATE_COLS), BF16),
    )
    return pl.pallas_call(
        _inproj_body,
        grid=(n // TM_IN, nj),
        in_specs=[
            pl.BlockSpec((TM_IN, D_MODEL), row),
            pl.BlockSpec((D_MODEL, TN_IN), lambda i, j: (0, j)),
            _const_spec((1, SGU_DIM)),
            _const_spec((1, SGU_DIM)),
        ],
        out_specs=(
            pl.BlockSpec((TM_IN, POOL_DIM), row),
            pl.BlockSpec((TM_IN, CONV_DIM), row),
            pl.BlockSpec((TM_IN, CONV_DIM), row),
            pl.BlockSpec((TM_IN, SGU_DIM), row),
            pl.BlockSpec((TM_IN, SGU_DIM), row),
            pl.BlockSpec((TM_IN, TN_IN), lambda i, j: (i, jnp.maximum(j - 4, 0))),
        ),
        out_shape=out_shape,
        scratch_shapes=[pltpu.VMEM((TM_IN, D_MODEL), BF16)],
        compiler_params=pltpu.CompilerParams(
            dimension_semantics=("arbitrary", "arbitrary"), vmem_limit_bytes=VMEM_LIMIT),
        name="inproj",
    )(x, w_in_p, ln_g, ln_b)


def _mixer_body(nseq, seq_len, pos0, tiles_per_seq, has_hist, *refs):
    if has_hist:
        (x_ref, a_ref, cb_ref, cv_ref, v_ref, u_ref, ga_ref, gb_ref, gc_ref, hp_ref, hc_ref,
         pw_ref, ps_ref, bd_ref, sb_ref, cw_ref, wa_ref, wb_ref, wc_ref, wo_ref, lg_ref, lb_ref,
         _alias_ref, x1_ref, np_ref, nc_ref, p_scr, q_scr) = refs
    else:
        (x_ref, a_ref, cb_ref, cv_ref, v_ref, u_ref, ga_ref, gb_ref, gc_ref,
         pw_ref, ps_ref, bd_ref, sb_ref, cw_ref, wa_ref, wb_ref, wc_ref, wo_ref, lg_ref, lb_ref,
         x1_ref, np_ref, nc_ref, p_scr, q_scr) = refs
    L = seq_len
    R = nseq * L
    t = pl.program_id(0) % tiles_per_seq

    if has_hist:
        p_scr[:, 0:POOL_PAD, :] = hp_ref[...]
        q_scr[:, 0:CONV_PAD, :] = hc_ref[...]
    else:
        @pl.when(t == 0)
        def _():
            p_scr[:, 0:POOL_PAD, :] = jnp.zeros((nseq, POOL_PAD, POOL_DIM), F32)
            q_scr[:, 0:CONV_PAD, :] = jnp.zeros((nseq, CONV_PAD, CONV_DIM), F32)

        @pl.when(t != 0)
        def _():
            p_scr[:, 0:POOL_PAD, :] = p_scr[:, L:L + POOL_PAD, :]
            q_scr[:, 0:CONV_PAD, :] = q_scr[:, L:L + CONV_PAD, :]
    p_scr[:, POOL_PAD:, :] = a_ref[...].reshape(nseq, L, POOL_DIM)
    q_scr[:, CONV_PAD:, :] = cv_ref[...].reshape(nseq, L, CONV_DIM)

    pos = pos0 + t * L + lax.broadcasted_iota(jnp.int32, (1, L, 1), 1) + 1
    ya = []
    for g, win in enumerate(POOL_WINDOWS):
        cols = slice(g * POOL_GROUP_DIM, (g + 1) * POOL_GROUP_DIM)
        cur = p_scr[:, POOL_PAD:POOL_PAD + L, cols]
        s = cur
        for k in range(1, win):
            s = s + p_scr[:, POOL_PAD - k:POOL_PAD - k + L, cols]
        cnt = jnp.minimum(pos, win).astype(F32)
        pooled = (s / cnt - cur).reshape(R, POOL_GROUP_DIM).astype(BF16)
        ya.append(jnp.dot(pooled, pw_ref[g], preferred_element_type=F32) * ps_ref[:, cols])
    ya = jnp.concatenate(ya, axis=-1).astype(BF16)

    vb = v_ref[...].astype(BF16)
    yb = []
    for g in range(SGU_GROUPS):
        cols = slice(g * SGU_GROUP_DIM, (g + 1) * SGU_GROUP_DIM)
        s = jnp.dot(bd_ref[g], vb[:, cols], preferred_element_type=F32) + sb_ref[:, cols]
        yb.append(u_ref[:, cols].astype(F32) * s)
    yb = jnp.concatenate(yb, axis=-1).astype(BF16)

    conv = q_scr[:, CONV_PAD - 2:CONV_PAD - 2 + L, :] * cw_ref[0:1, :]
    for k in range(1, CONV_WIDTH):
        o = CONV_PAD - 2 + k
        conv = conv + q_scr[:, o:o + L, :] * cw_ref[k:k + 1, :]
    yc = (cb_ref[...].astype(F32) * conv.reshape(R, CONV_DIM)).astype(BF16)

    merged = ga_ref[...].astype(F32) * jnp.dot(ya, wa_ref[...], preferred_element_type=F32)
    merged = merged + gb_ref[...].astype(F32) * jnp.dot(yb, wb_ref[...], preferred_element_type=F32)
    merged = merged + gc_ref[...].astype(F32) * jnp.dot(yc, wc_ref[...], preferred_element_type=F32)
    o = jnp.dot(merged.astype(BF16), wo_ref[...], preferred_element_type=F32)
    x1_ref[...] = _layer_norm(ALPHA * x_ref[...] + o, lg_ref[...], lb_ref[...])

    if has_hist:
        np_ref[...] = p_scr[:, L:L + POOL_PAD, :]
        nc_ref[...] = q_scr[:, L:L + CONV_PAD, :]
    else:
        @pl.when(t == tiles_per_seq - 1)
        def _():
            np_ref[...] = p_scr[:, L:L + POOL_PAD, :]
            nc_ref[...] = q_scr[:, L:L + CONV_PAD, :]


def _mixer(x, acts, lw, *, row0, n_rows, nseq, seq_len, pos0, tiles_per_seq, hist=None, x1_prev=None):
    a, cb, cv, v, u, gates = acts
    R = nseq * seq_len
    assert R == TM_MIX and row0 % R == 0 and n_rows % R == 0
    n_tiles = n_rows // R
    off = row0 // R
    row = lambda i: (i + off, 0)
    has_hist = hist is not None
    n_state_seq = n_tiles * nseq // tiles_per_seq
    state_idx = lambda i: (i // tiles_per_seq, 0, 0)

    in_specs = [
        pl.BlockSpec((R, D_MODEL), row),
        pl.BlockSpec((R, POOL_DIM), row),
        pl.BlockSpec((R, CONV_DIM), row),
        pl.BlockSpec((R, CONV_DIM), row),
        pl.BlockSpec((R, SGU_DIM), row),
        pl.BlockSpec((R, SGU_DIM), row),
        pl.BlockSpec((R, D_MODEL), lambda i: (i + off, 0)),
        pl.BlockSpec((R, D_MODEL), lambda i: (i + off, 1)),
        pl.BlockSpec((R, D_MODEL), lambda i: (i + off, 2)),
    ]
    args = [x, a, cb, cv, v, u, gates, gates, gates]
    if has_hist:
        hp, hc = hist
        in_specs += [pl.BlockSpec((nseq, POOL_PAD, POOL_DIM), state_idx),
                     pl.BlockSpec((nseq, CONV_PAD, CONV_DIM), state_idx)]
        args += [hp, hc]
    weights = [lw["pool_w"], lw["pool_scale"], lw["sgu_bd"][R], lw["sgu_bias"][R], lw["conv_w"],
               lw["w_br_a"], lw["w_br_b"], lw["w_br_c"], lw["w_o"], lw["ln1_g"], lw["ln1_b"]]
    in_specs += [_const_spec(w.shape) for w in weights]
    args += weights
    aliases = {}
    if x1_prev is not None:
        in_specs.append(pl.BlockSpec(memory_space=pl.ANY))
        args.append(x1_prev)
        aliases = {len(args) - 1: 0}

    out_shape = (
        jax.ShapeDtypeStruct(x.shape, F32),
        jax.ShapeDtypeStruct((n_state_seq, POOL_PAD, POOL_DIM), F32),
        jax.ShapeDtypeStruct((n_state_seq, CONV_PAD, CONV_DIM), F32),
    )
    out_specs = (
        pl.BlockSpec((R, D_MODEL), row),
        pl.BlockSpec((nseq, POOL_PAD, POOL_DIM), state_idx),
        pl.BlockSpec((nseq, CONV_PAD, CONV_DIM), state_idx),
    )
    return pl.pallas_call(
        functools.partial(_mixer_body, nseq, seq_len, pos0, tiles_per_seq, has_hist),
        grid=(n_tiles,),
        in_specs=in_specs,
        out_specs=out_specs,
        out_shape=out_shape,
        scratch_shapes=[pltpu.VMEM((nseq, POOL_PAD + seq_len, POOL_DIM), F32),
                        pltpu.VMEM((nseq, CONV_PAD + seq_len, CONV_DIM), F32)],
        input_output_aliases=aliases,
        compiler_params=pltpu.CompilerParams(
            dimension_semantics=("arbitrary",), vmem_limit_bytes=VMEM_LIMIT),
        name="mixer_sample" if has_hist else "mixer_prompt",
    )(*args)


def _ffn_body(x_ref, p_ref, wg_ref, wu_ref, wd_ref, wpg_ref, wpe_ref, lg_ref, lb_ref,
              o_ref, xb_ref, acc_ref):
    j = pl.program_id(1)

    @pl.when(j == 0)
    def _():
        x = x_ref[...]
        xb = x.astype(BF16)
        xb_ref[...] = xb
        gate = jax.nn.sigmoid(jnp.dot(xb, wpg_ref[...], preferred_element_type=F32))
        emb = jnp.dot(p_ref[...].astype(BF16), wpe_ref[...], preferred_element_type=F32)
        acc_ref[...] = ALPHA * x + gate * emb

    xb = xb_ref[...]
    hg = jnp.dot(xb, wg_ref[...], preferred_element_type=F32)
    hu = jnp.dot(xb, wu_ref[...], preferred_element_type=F32)
    act = (hg * jax.nn.sigmoid(hg) * hu).astype(BF16)
    acc_ref[...] += jnp.dot(act, wd_ref[...], preferred_element_type=F32)

    @pl.when(j == pl.num_programs(1) - 1)
    def _():
        o_ref[...] = _layer_norm(acc_ref[...], lg_ref[...], lb_ref[...])


def _ffn(x1, p, lw):
    n = x1.shape[0]
    nj = D_FF // TF_FFN
    row = lambda i, j: (i, 0)
    return pl.pallas_call(
        _ffn_body,
        grid=(n // TM_FFN, nj),
        in_specs=[
            pl.BlockSpec((TM_FFN, D_MODEL), row),
            pl.BlockSpec((TM_FFN, PLE_DIM), row),
            pl.BlockSpec((D_MODEL, TF_FFN), lambda i, j: (0, j)),
            pl.BlockSpec((D_MODEL, TF_FFN), lambda i, j: (0, j + nj)),
            pl.BlockSpec((TF_FFN, D_MODEL), lambda i, j: (j, 0)),
            _const_spec((D_MODEL, D_MODEL)),
            _const_spec((PLE_DIM, D_MODEL)),
            _const_spec((1, D_MODEL)),
            _const_spec((1, D_MODEL)),
        ],
        out_specs=pl.BlockSpec((TM_FFN, D_MODEL), row),
        out_shape=jax.ShapeDtypeStruct((n, D_MODEL), F32),
        scratch_shapes=[pltpu.VMEM((TM_FFN, D_MODEL), BF16), pltpu.VMEM((TM_FFN, D_MODEL), F32)],
        compiler_params=pltpu.CompilerParams(
            dimension_semantics=("arbitrary", "arbitrary"), vmem_limit_bytes=VMEM_LIMIT),
        name="ffn",
    )(x1, p, lw["w_gu"], lw["w_gu"], lw["w_down"], lw["w_pe_gate"], lw["w_pe"],
      lw["ln2_g"], lw["ln2_b"])


def _sgu_block_diag(sgu_w, sgu_b, chunk, rows):
    w = jnp.tril(sgu_w[:, :chunk, :chunk])
    reps = rows // chunk
    eye = jnp.eye(reps, dtype=w.dtype)
    bd = jnp.einsum("ab,gts->gatbs", eye, w).reshape(SGU_GROUPS, rows, rows)
    bias = jnp.tile(jnp.transpose(sgu_b[:, :chunk]), (reps, 1))
    bias = jnp.repeat(bias, SGU_GROUP_DIM, axis=1)
    return bd.astype(BF16), bias.astype(F32)


def _layer_weights(i, w_in, pool_w, pool_scale, sgu_ln_g, sgu_ln_b, sgu_w, sgu_b, conv_w, w_br_a,
                   w_br_b, w_br_c, w_o, ln1_g, ln1_b, w_gu, w_down, w_pe, w_pe_gate, ln2_g, ln2_b,
                   sample_len):
    off_u, off_v = POOL_DIM, POOL_DIM + SGU_DIM
    off_cb = off_v + SGU_DIM
    off_g = off_cb + 3 * CONV_DIM
    wi = w_in[i]
    w_in_p = jnp.concatenate(
        [wi[:, off_u:off_v], wi[:, off_v:off_cb], wi[:, :POOL_DIM], wi[:, off_cb:off_g],
         wi[:, off_g:]], axis=1).astype(BF16)
    bd_p, bias_p = _sgu_block_diag(sgu_w[i], sgu_b[i], SGU_CHUNK, TM_MIX)
    bd_s, bias_s = _sgu_block_diag(sgu_w[i], sgu_b[i], sample_len, TM_MIX)
    row = lambda a: a[i].reshape(1, -1).astype(F32)
    return dict(
        w_in=w_in_p, sgu_ln_g=row(sgu_ln_g), sgu_ln_b=row(sgu_ln_b),
        pool_w=pool_w[i].astype(BF16), pool_scale=row(pool_scale),
        sgu_bd={"prompt": bd_p, "sample": bd_s}, sgu_bias={"prompt": bias_p, "sample": bias_s},
        conv_w=conv_w[i].astype(F32),
        w_br_a=w_br_a[i].astype(BF16), w_br_b=w_br_b[i].astype(BF16), w_br_c=w_br_c[i].astype(BF16),
        w_o=w_o[i].astype(BF16), ln1_g=row(ln1_g), ln1_b=row(ln1_b),
        w_gu=w_gu[i].astype(BF16), w_down=w_down[i].astype(BF16),
        w_pe=w_pe[i].astype(BF16), w_pe_gate=w_pe_gate[i].astype(BF16),
        ln2_g=row(ln2_g), ln2_b=row(ln2_b),
    )


def kernel(x_prompt, x_sample, state_pool, state_conv, p_prompt, p_sample, w_in, pool_w, pool_scale,
           sgu_ln_g, sgu_ln_b, sgu_w, sgu_b, conv_w, w_br_a, w_br_b, w_br_c, w_o, ln1_g, ln1_b, w_gu,
           w_down, w_pe, w_pe_gate, ln2_g, ln2_b):
    batch, seq, _ = x_prompt.shape
    dec_batch, dec_seq, _ = x_sample.shape
    n_p = batch * seq
    n_s = dec_batch * dec_seq
    assert seq % TM_MIX == 0 and TM_MIX % dec_seq == 0 and n_s % TM_MIX == 0
    assert (n_p + n_s) % TM_IN == 0 and (n_p + n_s) % TM_FFN == 0 and n_p % TM_MIX == 0
    seq_per_tile = TM_MIX // dec_seq

    x = jnp.concatenate([x_prompt.reshape(n_p, D_MODEL), x_sample.reshape(n_s, D_MODEL)], axis=0)
    hist_pool = jnp.pad(state_pool, ((0, 0), (0, 0), (POOL_PAD - POOL_HIST, 0), (0, 0)))
    hist_conv = jnp.pad(state_conv, ((0, 0), (0, 0), (CONV_PAD - (CONV_WIDTH - 1), 0), (0, 0)))

    outs = {k: [] for k in ("pool_p", "conv_p", "pool_s", "conv_s", "v_s")}
    for i in range(DEPTH):
        lw = _layer_weights(i, w_in, pool_w, pool_scale, sgu_ln_g, sgu_ln_b, sgu_w, sgu_b, conv_w,
                            w_br_a, w_br_b, w_br_c, w_o, ln1_g, ln1_b, w_gu, w_down, w_pe,
                            w_pe_gate, ln2_g, ln2_b, dec_seq)
        p = jnp.concatenate([p_prompt[i].reshape(n_p, PLE_DIM), p_sample[i].reshape(n_s, PLE_DIM)],
                            axis=0)
        acts = _inproj(x, lw["w_in"], lw["sgu_ln_g"], lw["sgu_ln_b"])

        lw_p = dict(lw, sgu_bd={TM_MIX: lw["sgu_bd"]["prompt"]},
                    sgu_bias={TM_MIX: lw["sgu_bias"]["prompt"]})
        lw_s = dict(lw, sgu_bd={TM_MIX: lw["sgu_bd"]["sample"]},
                    sgu_bias={TM_MIX: lw["sgu_bias"]["sample"]})
        x1, pool_p, conv_p = _mixer(
            x, acts, lw_p, row0=0, n_rows=n_p, nseq=1, seq_len=TM_MIX, pos0=0,
            tiles_per_seq=seq // TM_MIX)
        x1, pool_s, conv_s = _mixer(
            x, acts, lw_s, row0=n_p, n_rows=n_s, nseq=seq_per_tile, seq_len=dec_seq,
            pos0=CHUNK_POS0_SAMPLE, tiles_per_seq=1, hist=(hist_pool[i], hist_conv[i]), x1_prev=x1)
        x = _ffn(x1, p, lw)

        outs["pool_p"].append(pool_p[:, POOL_PAD - POOL_HIST:])
        outs["conv_p"].append(conv_p[:, CONV_PAD - (CONV_WIDTH - 1):])
        outs["pool_s"].append(pool_s[:, POOL_PAD - POOL_HIST:])
        outs["conv_s"].append(conv_s[:, CONV_PAD - (CONV_WIDTH - 1):])
        outs["v_s"].append(acts[3][n_p:].reshape(dec_batch, dec_seq, SGU_DIM))

    return (x[:n_p].reshape(batch, seq, D_MODEL), x[n_p:].reshape(dec_batch, dec_seq, D_MODEL),
            jnp.stack(outs["pool_p"]), jnp.stack(outs["conv_p"]), jnp.stack(outs["pool_s"]),
            jnp.stack(outs["conv_s"]), jnp.stack(outs["v_s"]))
```

```python
import functools

import jax
import jax.numpy as jnp
from jax import lax
from jax.experimental import pallas as pl
from jax.experimental.pallas import tpu as pltpu

F32 = jnp.float32
BF16 = jnp.bfloat16

D_MODEL = 2048
DEPTH = 2
POOL_DIM = 512
POOL_WINDOWS = (2, 4, 8, 16)
POOL_GROUP_DIM = 128
POOL_HIST = 15
POOL_PAD = 16
SGU_DIM = 1024
SGU_GROUPS = 8
SGU_GROUP_DIM = 128
SGU_CHUNK = 128
CONV_DIM = 512
CONV_WIDTH = 3
CONV_PAD = 8
D_FF = 5632
PLE_DIM = 256
ALPHA = (2 * DEPTH) ** 0.25
LN_EPS = 1e-5
IN_COLS = 10240
GATE_COLS = 3 * D_MODEL

TN_IN = 1024
NC_IN = 256
MC_IN = 512
TM_MIX = 256
TM_FFN = 512
TF_FFN = 512
VMEM_LIMIT = 56 * 1024 * 1024


def _layer_norm(x, g, b):
    mu = jnp.mean(x, axis=-1, keepdims=True)
    xc = x - mu
    var = jnp.mean(xc * xc, axis=-1, keepdims=True)
    return xc * lax.rsqrt(var + LN_EPS) * g + b


def _gelu(x):
    return 0.5 * x * (1.0 + lax.erf(x * (2.0 ** -0.5)))


def _const_spec(shape):
    nd = len(shape)
    return pl.BlockSpec(shape, lambda *_: (0,) * nd, pipeline_mode=pl.Buffered(1))


def _sigmoid(x):
    return 0.5 * jnp.tanh(0.5 * x) + 0.5


def _inproj_body(x_ref, w_ref, lng_ref, lnb_ref,
                 a_ref, cb_ref, conv_ref, v_ref, u_ref, g_ref):
    j = pl.program_id(1)
    tm = x_ref.shape[0]
    mc = min(MC_IN, tm)
    nchunks = TN_IN // NC_IN
    half = nchunks // 2
    row_chunks = [pl.ds(m, mc) for m in range(0, tm, mc)]

    def cols(n):
        return slice(n * NC_IN, (n + 1) * NC_IN)

    def z(rows, n):
        return jnp.dot(x_ref[rows, :], w_ref[:, cols(n)], preferred_element_type=F32)

    @pl.when(j == 0)
    def _():
        for rows in row_chunks:
            for n in range(nchunks):
                u_ref[rows, cols(n)] = _gelu(z(rows, n)).astype(u_ref.dtype)

    @pl.when(j == 1)
    def _():
        for rows in row_chunks:
            v = jnp.concatenate([_gelu(z(rows, n)) for n in range(nchunks)], axis=-1)
            v_ref[rows, :] = _layer_norm(v, lng_ref[...], lnb_ref[...]).astype(v_ref.dtype)

    @pl.when(j == 2)
    def _():
        for rows in row_chunks:
            for n in range(half):
                a_ref[rows, cols(n)] = z(rows, n)
            for n in range(half):
                cb_ref[rows, cols(n)] = z(rows, half + n).astype(cb_ref.dtype)

    @pl.when(j == 3)
    def _():
        for rows in row_chunks:
            for n in range(half):
                conv_ref[rows, cols(n)] = z(rows, n) * z(rows, half + n)

    @pl.when(j >= 4)
    def _():
        for rows in row_chunks:
            for n in range(nchunks):
                g_ref[rows, cols(n)] = _sigmoid(z(rows, n)).astype(g_ref.dtype)


def _inproj(xb, w_in_p, ln_g, ln_b, *, tm, v_dtype):
    n = xb.shape[0]
    assert n % tm == 0
    n_tiles = n // tm
    nj = IN_COLS // TN_IN

    def early(k):
        return lambda i, j: (jnp.minimum(i + (j > k).astype(jnp.int32), n_tiles - 1), 0)

    out_shape = (
        jax.ShapeDtypeStruct((n, POOL_DIM), F32),
        jax.ShapeDtypeStruct((n, CONV_DIM), BF16),
        jax.ShapeDtypeStruct((n, CONV_DIM), F32),
        jax.ShapeDtypeStruct((n, SGU_DIM), v_dtype),
        jax.ShapeDtypeStruct((n, SGU_DIM), BF16),
        jax.ShapeDtypeStruct((n, GATE_COLS), BF16),
    )
    return pl.pallas_call(
        _inproj_body,
        grid=(n_tiles, nj),
        in_specs=[
            pl.BlockSpec((tm, D_MODEL), lambda i, j: (i, 0)),
            pl.BlockSpec((D_MODEL, TN_IN), lambda i, j: (0, j)),
            _const_spec((1, SGU_DIM)),
            _const_spec((1, SGU_DIM)),
        ],
        out_specs=(
            pl.BlockSpec((tm, POOL_DIM), early(2)),
            pl.BlockSpec((tm, CONV_DIM), early(2)),
            pl.BlockSpec((tm, CONV_DIM), early(3)),
            pl.BlockSpec((tm, SGU_DIM), early(1)),
            pl.BlockSpec((tm, SGU_DIM), early(0)),
            pl.BlockSpec((tm, TN_IN), lambda i, j: (i, jnp.maximum(j - 4, 0))),
        ),
        out_shape=out_shape,
        compiler_params=pltpu.CompilerParams(
            dimension_semantics=("arbitrary", "arbitrary"), vmem_limit_bytes=VMEM_LIMIT),
        name="inproj",
    )(xb, w_in_p, ln_g, ln_b)


def _mixer_body(nseq, seq_len, pos0, tiles_per_seq, has_hist, *refs):
    if has_hist:
        (x_ref, a_ref, cb_ref, cv_ref, v_ref, u_ref, ga_ref, gb_ref, gc_ref, hp_ref, hc_ref,
         pw_ref, ps_ref, bd_ref, sb_ref, cw_ref, wa_ref, wb_ref, wc_ref, wo_ref, lg_ref, lb_ref,
         x1_ref, x1b_ref, np_ref, nc_ref, p_scr, q_scr) = refs
    else:
        (x_ref, a_ref, cb_ref, cv_ref, v_ref, u_ref, ga_ref, gb_ref, gc_ref,
         pw_ref, ps_ref, bd_ref, sb_ref, cw_ref, wa_ref, wb_ref, wc_ref, wo_ref, lg_ref, lb_ref,
         x1_ref, x1b_ref, np_ref, nc_ref, p_scr, q_scr) = refs
    L = seq_len
    R = nseq * L
    t = pl.program_id(0) % tiles_per_seq

    if has_hist:
        p_scr[:, 0:POOL_PAD, :] = hp_ref[...]
        q_scr[:, 0:CONV_PAD, :] = hc_ref[...]
    else:
        @pl.when(t == 0)
        def _():
            p_scr[:, 0:POOL_PAD, :] = jnp.zeros((nseq, POOL_PAD, POOL_DIM), F32)
            q_scr[:, 0:CONV_PAD, :] = jnp.zeros((nseq, CONV_PAD, CONV_DIM), F32)

        @pl.when(t != 0)
        def _():
            p_scr[:, 0:POOL_PAD, :] = p_scr[:, L:L + POOL_PAD, :]
            q_scr[:, 0:CONV_PAD, :] = q_scr[:, L:L + CONV_PAD, :]
    p_scr[:, POOL_PAD:, :] = a_ref[...].reshape(nseq, L, POOL_DIM)
    q_scr[:, CONV_PAD:, :] = cv_ref[...].reshape(nseq, L, CONV_DIM)

    pos = pos0 + t * L + lax.broadcasted_iota(jnp.int32, (1, L, 1), 1) + 1
    ya = []
    for g, win in enumerate(POOL_WINDOWS):
        cols = slice(g * POOL_GROUP_DIM, (g + 1) * POOL_GROUP_DIM)
        cur = p_scr[:, POOL_PAD:POOL_PAD + L, cols]
        s = cur
        for k in range(1, win):
            s = s + p_scr[:, POOL_PAD - k:POOL_PAD - k + L, cols]
        cnt = jnp.minimum(pos, win).astype(F32)
        pooled = (s / cnt - cur).reshape(R, POOL_GROUP_DIM).astype(BF16)
        ya.append(jnp.dot(pooled, pw_ref[g], preferred_element_type=F32) * ps_ref[:, cols])
    ya = jnp.concatenate(ya, axis=-1).astype(BF16)

    vb = v_ref[...].astype(BF16)
    yb = []
    for g in range(SGU_GROUPS):
        cols = slice(g * SGU_GROUP_DIM, (g + 1) * SGU_GROUP_DIM)
        s = jnp.dot(bd_ref[g], vb[:, cols], preferred_element_type=F32) + sb_ref[:, cols]
        yb.append(u_ref[:, cols].astype(F32) * s)
    yb = jnp.concatenate(yb, axis=-1).astype(BF16)

    conv = q_scr[:, CONV_PAD - 2:CONV_PAD - 2 + L, :] * cw_ref[0:1, :]
    for k in range(1, CONV_WIDTH):
        o = CONV_PAD - 2 + k
        conv = conv + q_scr[:, o:o + L, :] * cw_ref[k:k + 1, :]
    yc = (cb_ref[...].astype(F32) * conv.reshape(R, CONV_DIM)).astype(BF16)

    merged = ga_ref[...].astype(F32) * jnp.dot(ya, wa_ref[...], preferred_element_type=F32)
    merged = merged + gb_ref[...].astype(F32) * jnp.dot(yb, wb_ref[...], preferred_element_type=F32)
    merged = merged + gc_ref[...].astype(F32) * jnp.dot(yc, wc_ref[...], preferred_element_type=F32)
    o = jnp.dot(merged.astype(BF16), wo_ref[...], preferred_element_type=F32)
    x1 = _layer_norm(ALPHA * x_ref[...] + o, lg_ref[...], lb_ref[...])
    x1_ref[...] = x1
    x1b_ref[...] = x1.astype(BF16)

    if has_hist:
        np_ref[...] = p_scr[:, L:L + POOL_PAD, :]
        nc_ref[...] = q_scr[:, L:L + CONV_PAD, :]
    else:
        @pl.when(t == tiles_per_seq - 1)
        def _():
            np_ref[...] = p_scr[:, L:L + POOL_PAD, :]
            nc_ref[...] = q_scr[:, L:L + CONV_PAD, :]


def _mixer(x, acts, lw, sgu_bd, sgu_bias, *, nseq, seq_len, pos0, tiles_per_seq, hist=None):
    a, cb, cv, v, u, gates = acts
    n = x.shape[0]
    R = nseq * seq_len
    assert R == TM_MIX and n % R == 0
    n_tiles = n // R
    row = lambda i: (i, 0)
    has_hist = hist is not None
    n_state_seq = n_tiles * nseq // tiles_per_seq
    state_idx = lambda i: (i // tiles_per_seq, 0, 0)

    in_specs = [
        pl.BlockSpec((R, D_MODEL), row),
        pl.BlockSpec((R, POOL_DIM), row),
        pl.BlockSpec((R, CONV_DIM), row),
        pl.BlockSpec((R, CONV_DIM), row),
        pl.BlockSpec((R, SGU_DIM), row),
        pl.BlockSpec((R, SGU_DIM), row),
        pl.BlockSpec((R, D_MODEL), lambda i: (i, 0)),
        pl.BlockSpec((R, D_MODEL), lambda i: (i, 1)),
        pl.BlockSpec((R, D_MODEL), lambda i: (i, 2)),
    ]
    args = [x, a, cb, cv, v, u, gates, gates, gates]
    if has_hist:
        hp, hc = hist
        in_specs += [pl.BlockSpec((nseq, POOL_PAD, POOL_DIM), state_idx),
                     pl.BlockSpec((nseq, CONV_PAD, CONV_DIM), state_idx)]
        args += [hp, hc]
    weights = [lw["pool_w"], lw["pool_scale"], sgu_bd, sgu_bias, lw["conv_w"],
               lw["w_br_a"], lw["w_br_b"], lw["w_br_c"], lw["w_o"], lw["ln1_g"], lw["ln1_b"]]
    in_specs += [_const_spec(w.shape) for w in weights]
    args += weights

    out_shape = (
        jax.ShapeDtypeStruct((n, D_MODEL), F32),
        jax.ShapeDtypeStruct((n, D_MODEL), BF16),
        jax.ShapeDtypeStruct((n_state_seq, POOL_PAD, POOL_DIM), F32),
        jax.ShapeDtypeStruct((n_state_seq, CONV_PAD, CONV_DIM), F32),
    )
    out_specs = (
        pl.BlockSpec((R, D_MODEL), row),
        pl.BlockSpec((R, D_MODEL), row),
        pl.BlockSpec((nseq, POOL_PAD, POOL_DIM), state_idx),
        pl.BlockSpec((nseq, CONV_PAD, CONV_DIM), state_idx),
    )
    return pl.pallas_call(
        functools.partial(_mixer_body, nseq, seq_len, pos0, tiles_per_seq, has_hist),
        grid=(n_tiles,),
        in_specs=in_specs,
        out_specs=out_specs,
        out_shape=out_shape,
        scratch_shapes=[pltpu.VMEM((nseq, POOL_PAD + seq_len, POOL_DIM), F32),
                        pltpu.VMEM((nseq, CONV_PAD + seq_len, CONV_DIM), F32)],
        compiler_params=pltpu.CompilerParams(
            dimension_semantics=("arbitrary",), vmem_limit_bytes=VMEM_LIMIT),
        name="mixer_sample" if has_hist else "mixer_prompt",
    )(*args)


def _ffn_body(x_ref, xb_ref, p_ref, wg_ref, wu_ref, wd_ref, wpg_ref, wpe_ref, lg_ref, lb_ref,
              o_ref, ob_ref):
    j = pl.program_id(1)

    @pl.when(j == 0)
    def _():
        gate = _sigmoid(jnp.dot(xb_ref[...], wpg_ref[...], preferred_element_type=F32))
        emb = jnp.dot(p_ref[...].astype(BF16), wpe_ref[...], preferred_element_type=F32)
        o_ref[...] = ALPHA * x_ref[...] + gate * emb

    xb = xb_ref[...]
    hg = jnp.dot(xb, wg_ref[...], preferred_element_type=F32)
    hu = jnp.dot(xb, wu_ref[...], preferred_element_type=F32)
    act = (hg * _sigmoid(hg) * hu).astype(BF16)
    o_ref[...] += jnp.dot(act, wd_ref[...], preferred_element_type=F32)

    @pl.when(j == pl.num_programs(1) - 1)
    def _():
        y = _layer_norm(o_ref[...], lg_ref[...], lb_ref[...])
        o_ref[...] = y
        ob_ref[...] = y.astype(BF16)


def _ffn(x1, x1b, p_all, p_row0, lw, *, tm):
    n = x1.shape[0]
    assert n % tm == 0 and p_row0 % tm == 0
    nj = D_FF // TF_FFN
    p_off = p_row0 // tm
    row = lambda i, j: (i, 0)
    return pl.pallas_call(
        _ffn_body,
        grid=(n // tm, nj),
        in_specs=[
            pl.BlockSpec((tm, D_MODEL), row),
            pl.BlockSpec((tm, D_MODEL), row),
            pl.BlockSpec((tm, PLE_DIM), lambda i, j: (i + p_off, 0)),
            pl.BlockSpec((D_MODEL, TF_FFN), lambda i, j: (0, j)),
            pl.BlockSpec((D_MODEL, TF_FFN), lambda i, j: (0, j + nj)),
            pl.BlockSpec((TF_FFN, D_MODEL), lambda i, j: (j, 0)),
            _const_spec((D_MODEL, D_MODEL)),
            _const_spec((PLE_DIM, D_MODEL)),
            _const_spec((1, D_MODEL)),
            _const_spec((1, D_MODEL)),
        ],
        out_specs=(pl.BlockSpec((tm, D_MODEL), row), pl.BlockSpec((tm, D_MODEL), row)),
        out_shape=(jax.ShapeDtypeStruct((n, D_MODEL), F32), jax.ShapeDtypeStruct((n, D_MODEL), BF16)),
        compiler_params=pltpu.CompilerParams(
            dimension_semantics=("arbitrary", "arbitrary"), vmem_limit_bytes=VMEM_LIMIT),
        name="ffn",
    )(x1, x1b, p_all, lw["w_gu"], lw["w_gu"], lw["w_down"], lw["w_pe_gate"], lw["w_pe"],
      lw["ln2_g"], lw["ln2_b"])


def _sgu_block_diag(sgu_w, sgu_b, chunk, rows):
    w = jnp.tril(sgu_w[:, :chunk, :chunk])
    reps = rows // chunk
    eye = jnp.eye(reps, dtype=w.dtype)
    bd = jnp.einsum("ab,gts->gatbs", eye, w).reshape(SGU_GROUPS, rows, rows)
    bias = jnp.tile(jnp.transpose(sgu_b[:, :chunk]), (reps, 1))
    bias = jnp.repeat(bias, SGU_GROUP_DIM, axis=1)
    return bd.astype(BF16), bias.astype(F32)


def _layer_weights(i, w_in, pool_w, pool_scale, sgu_ln_g, sgu_ln_b, sgu_w, sgu_b, conv_w, w_br_a,
                   w_br_b, w_br_c, w_o, ln1_g, ln1_b, w_gu, w_down, w_pe, w_pe_gate, ln2_g, ln2_b,
                   sample_len):
    off_u, off_v = POOL_DIM, POOL_DIM + SGU_DIM
    off_cb = off_v + SGU_DIM
    off_g = off_cb + 3 * CONV_DIM
    wi = w_in[i]
    w_in_p = jnp.concatenate(
        [wi[:, off_u:off_v], wi[:, off_v:off_cb], wi[:, :POOL_DIM], wi[:, off_cb:off_g],
         wi[:, off_g:]], axis=1).astype(BF16)
    row = lambda a: a[i].reshape(1, -1).astype(F32)
    return dict(
        w_in=w_in_p, sgu_ln_g=row(sgu_ln_g), sgu_ln_b=row(sgu_ln_b),
        pool_w=pool_w[i].astype(BF16), pool_scale=row(pool_scale),
        sgu_prompt=_sgu_block_diag(sgu_w[i], sgu_b[i], SGU_CHUNK, TM_MIX),
        sgu_sample=_sgu_block_diag(sgu_w[i], sgu_b[i], sample_len, TM_MIX),
        conv_w=conv_w[i].astype(F32),
        w_br_a=w_br_a[i].astype(BF16), w_br_b=w_br_b[i].astype(BF16), w_br_c=w_br_c[i].astype(BF16),
        w_o=w_o[i].astype(BF16), ln1_g=row(ln1_g), ln1_b=row(ln1_b),
        w_gu=w_gu[i].astype(BF16), w_down=w_down[i].astype(BF16),
        w_pe=w_pe[i].astype(BF16), w_pe_gate=w_pe_gate[i].astype(BF16),
        ln2_g=row(ln2_g), ln2_b=row(ln2_b),
    )


def kernel(x_prompt, x_sample, state_pool, state_conv, p_prompt, p_sample, w_in, pool_w, pool_scale,
           sgu_ln_g, sgu_ln_b, sgu_w, sgu_b, conv_w, w_br_a, w_br_b, w_br_c, w_o, ln1_g, ln1_b, w_gu,
           w_down, w_pe, w_pe_gate, ln2_g, ln2_b):
    batch, seq, _ = x_prompt.shape
    dec_batch, dec_seq, _ = x_sample.shape
    n_p = batch * seq
    n_s = dec_batch * dec_seq
    tm_in_p = 1024
    assert seq % TM_MIX == 0 and TM_MIX % dec_seq == 0 and n_s % TM_MIX == 0
    assert n_p % tm_in_p == 0 and n_p % TM_FFN == 0

    xp = x_prompt.reshape(n_p, D_MODEL)
    xs = x_sample.reshape(n_s, D_MODEL)
    xpb = xp.astype(BF16)
    xsb = xs.astype(BF16)
    pp_all = p_prompt.reshape(DEPTH * n_p, PLE_DIM)
    ps_all = p_sample.reshape(DEPTH * n_s, PLE_DIM)
    hist_pool = jnp.pad(state_pool, ((0, 0), (0, 0), (POOL_PAD - POOL_HIST, 0), (0, 0)))
    hist_conv = jnp.pad(state_conv, ((0, 0), (0, 0), (CONV_PAD - (CONV_WIDTH - 1), 0), (0, 0)))

    outs = {k: [] for k in ("pool_p", "conv_p", "pool_s", "conv_s", "v_s")}
    for i in range(DEPTH):
        lw = _layer_weights(i, w_in, pool_w, pool_scale, sgu_ln_g, sgu_ln_b, sgu_w, sgu_b, conv_w,
                            w_br_a, w_br_b, w_br_c, w_o, ln1_g, ln1_b, w_gu, w_down, w_pe,
                            w_pe_gate, ln2_g, ln2_b, dec_seq)
        acts = _inproj(xpb, lw["w_in"], lw["sgu_ln_g"], lw["sgu_ln_b"], tm=tm_in_p, v_dtype=BF16)
        x1, x1b, pool_p, conv_p = _mixer(
            xp, acts, lw, *lw["sgu_prompt"], nseq=1, seq_len=TM_MIX, pos0=0,
            tiles_per_seq=seq // TM_MIX)
        xp, xpb = _ffn(x1, x1b, pp_all, i * n_p, lw, tm=TM_FFN)
        acts = _inproj(xsb, lw["w_in"], lw["sgu_ln_g"], lw["sgu_ln_b"], tm=n_s, v_dtype=F32)
        x1, x1b, pool_s, conv_s = _mixer(
            xs, acts, lw, *lw["sgu_sample"], nseq=TM_MIX // dec_seq, seq_len=dec_seq, pos0=1024,
            tiles_per_seq=1, hist=(hist_pool[i], hist_conv[i]))
        xs, xsb = _ffn(x1, x1b, ps_all, i * n_s, lw, tm=n_s)

        outs["pool_p"].append(pool_p[:, POOL_PAD - POOL_HIST:])
        outs["conv_p"].append(conv_p[:, CONV_PAD - (CONV_WIDTH - 1):])
        outs["pool_s"].append(pool_s[:, POOL_PAD - POOL_HIST:])
        outs["conv_s"].append(conv_s[:, CONV_PAD - (CONV_WIDTH - 1):])
        outs["v_s"].append(acts[3].reshape(dec_batch, dec_seq, SGU_DIM))

    return (xp.reshape(batch, seq, D_MODEL), xs.reshape(dec_batch, dec_seq, D_MODEL),
            jnp.stack(outs["pool_p"]), jnp.stack(outs["conv_p"]), jnp.stack(outs["pool_s"]),
            jnp.stack(outs["conv_s"]), jnp.stack(outs["v_s"]))
```

```python
import functools

import jax
import jax.numpy as jnp
from jax import lax
from jax.experimental import pallas as pl
from jax.experimental.pallas import tpu as pltpu

F32 = jnp.float32
BF16 = jnp.bfloat16

D_MODEL = 2048
DEPTH = 2
POOL_DIM = 512
POOL_WINDOWS = (2, 4, 8, 16)
POOL_GROUP_DIM = 128
POOL_HIST = 15
POOL_PAD = 16
SGU_DIM = 1024
SGU_GROUPS = 8
SGU_GROUP_DIM = 128
SGU_CHUNK = 128
CONV_DIM = 512
CONV_WIDTH = 3
CONV_PAD = 8
D_FF = 5632
PLE_DIM = 256
ALPHA = (2 * DEPTH) ** 0.25
LN_EPS = 1e-5
IN_COLS = 10240
GATE_COLS = 3 * D_MODEL

TN_IN = 1024
NC_IN = 256
MC_IN = 512
MC_LN = 256
TM_MIX = 256
TM_FFN = 512
TF_FFN = 512
VMEM_LIMIT = 56 * 1024 * 1024


def _layer_norm(x, g, b):
    mu = jnp.mean(x, axis=-1, keepdims=True)
    xc = x - mu
    var = jnp.mean(xc * xc, axis=-1, keepdims=True)
    return xc * lax.rsqrt(var + LN_EPS) * g + b


def _gelu(x):
    return 0.5 * x * (1.0 + lax.erf(x * (2.0 ** -0.5)))


def _const_spec(shape):
    nd = len(shape)
    return pl.BlockSpec(shape, lambda *_: (0,) * nd, pipeline_mode=pl.Buffered(1))


def _sigmoid(x):
    return 0.5 * jnp.tanh(0.5 * x) + 0.5


def _inproj_body(x_ref, w_ref, lng_ref, lnb_ref,
                 a_ref, cb_ref, conv_ref, v_ref, u_ref, g_ref, *scratch):
    xb_ref = scratch[0] if scratch else x_ref
    j = pl.program_id(1)
    tm = x_ref.shape[0]
    mc = min(MC_IN, tm)
    nchunks = TN_IN // NC_IN
    half = nchunks // 2
    row_chunks = [pl.ds(m, mc) for m in range(0, tm, mc)]
    ln_chunks = [pl.ds(m, min(MC_LN, tm)) for m in range(0, tm, min(MC_LN, tm))]

    def cols(n):
        return slice(n * NC_IN, (n + 1) * NC_IN)

    def z(rows, n):
        return jnp.dot(xb_ref[rows, :], w_ref[:, cols(n)], preferred_element_type=F32)

    @pl.when(j == 0)
    def _():
        for rows in row_chunks:
            if scratch:
                xb_ref[rows, :] = x_ref[rows, :].astype(BF16)
            for n in range(nchunks):
                u_ref[rows, cols(n)] = _gelu(z(rows, n)).astype(u_ref.dtype)

    @pl.when(j == 1)
    def _():
        for rows in ln_chunks:
            v = jnp.concatenate([_gelu(z(rows, n)) for n in range(nchunks)], axis=-1)
            v_ref[rows, :] = _layer_norm(v, lng_ref[...], lnb_ref[...]).astype(v_ref.dtype)

    @pl.when(j == 2)
    def _():
        for rows in row_chunks:
            for n in range(half):
                a_ref[rows, cols(n)] = z(rows, n)
            for n in range(half):
                cb_ref[rows, cols(n)] = z(rows, half + n).astype(cb_ref.dtype)

    @pl.when(j == 3)
    def _():
        for rows in row_chunks:
            for n in range(half):
                conv_ref[rows, cols(n)] = z(rows, n) * z(rows, half + n)

    @pl.when(j >= 4)
    def _():
        for rows in row_chunks:
            for n in range(nchunks):
                g_ref[rows, cols(n)] = _sigmoid(z(rows, n)).astype(g_ref.dtype)


def _inproj(x, w_in_p, ln_g, ln_b, *, tm, v_dtype):
    n = x.shape[0]
    assert n % tm == 0
    n_tiles = n // tm
    nj = IN_COLS // TN_IN

    def early(k):
        return lambda i, j: (jnp.minimum(i + (j > k).astype(jnp.int32), n_tiles - 1), 0)

    out_shape = (
        jax.ShapeDtypeStruct((n, POOL_DIM), F32),
        jax.ShapeDtypeStruct((n, CONV_DIM), BF16),
        jax.ShapeDtypeStruct((n, CONV_DIM), F32),
        jax.ShapeDtypeStruct((n, SGU_DIM), v_dtype),
        jax.ShapeDtypeStruct((n, SGU_DIM), BF16),
        jax.ShapeDtypeStruct((n, GATE_COLS), BF16),
    )
    return pl.pallas_call(
        _inproj_body,
        grid=(n_tiles, nj),
        in_specs=[
            pl.BlockSpec((tm, D_MODEL), lambda i, j: (i, 0)),
            pl.BlockSpec((D_MODEL, TN_IN), lambda i, j: (0, j)),
            _const_spec((1, SGU_DIM)),
            _const_spec((1, SGU_DIM)),
        ],
        out_specs=(
            pl.BlockSpec((tm, POOL_DIM), early(2)),
            pl.BlockSpec((tm, CONV_DIM), early(2)),
            pl.BlockSpec((tm, CONV_DIM), early(3)),
            pl.BlockSpec((tm, SGU_DIM), early(1)),
            pl.BlockSpec((tm, SGU_DIM), early(0)),
            pl.BlockSpec((tm, TN_IN), lambda i, j: (i, jnp.maximum(j - 4, 0))),
        ),
        out_shape=out_shape,
        scratch_shapes=[] if x.dtype == BF16 else [pltpu.VMEM((tm, D_MODEL), BF16)],
        compiler_params=pltpu.CompilerParams(
            dimension_semantics=("arbitrary", "arbitrary"), vmem_limit_bytes=VMEM_LIMIT),
        name="inproj",
    )(x, w_in_p, ln_g, ln_b)


def _mixer_body(nseq, seq_len, pos0, tiles_per_seq, has_hist, *refs):
    if has_hist:
        (x_ref, a_ref, cb_ref, cv_ref, v_ref, u_ref, ga_ref, gb_ref, gc_ref, hp_ref, hc_ref,
         pw_ref, ps_ref, bd_ref, sb_ref, cw_ref, wa_ref, wb_ref, wc_ref, wo_ref, lg_ref, lb_ref,
         x1_ref, x1b_ref, np_ref, nc_ref, p_scr, q_scr) = refs
    else:
        (x_ref, a_ref, cb_ref, cv_ref, v_ref, u_ref, ga_ref, gb_ref, gc_ref,
         pw_ref, ps_ref, bd_ref, sb_ref, cw_ref, wa_ref, wb_ref, wc_ref, wo_ref, lg_ref, lb_ref,
         x1_ref, x1b_ref, np_ref, nc_ref, p_scr, q_scr) = refs
    L = seq_len
    R = nseq * L
    t = pl.program_id(0) % tiles_per_seq

    if has_hist:
        p_scr[:, 0:POOL_PAD, :] = hp_ref[...]
        q_scr[:, 0:CONV_PAD, :] = hc_ref[...]
    else:
        @pl.when(t == 0)
        def _():
            p_scr[:, 0:POOL_PAD, :] = jnp.zeros((nseq, POOL_PAD, POOL_DIM), F32)
            q_scr[:, 0:CONV_PAD, :] = jnp.zeros((nseq, CONV_PAD, CONV_DIM), F32)

        @pl.when(t != 0)
        def _():
            p_scr[:, 0:POOL_PAD, :] = p_scr[:, L:L + POOL_PAD, :]
            q_scr[:, 0:CONV_PAD, :] = q_scr[:, L:L + CONV_PAD, :]
    p_scr[:, POOL_PAD:, :] = a_ref[...].reshape(nseq, L, POOL_DIM)
    q_scr[:, CONV_PAD:, :] = cv_ref[...].reshape(nseq, L, CONV_DIM)

    pos = pos0 + t * L + lax.broadcasted_iota(jnp.int32, (1, L, 1), 1) + 1
    ya = []
    for g, win in enumerate(POOL_WINDOWS):
        cols = slice(g * POOL_GROUP_DIM, (g + 1) * POOL_GROUP_DIM)
        cur = p_scr[:, POOL_PAD:POOL_PAD + L, cols]
        s = cur
        for k in range(1, win):
            s = s + p_scr[:, POOL_PAD - k:POOL_PAD - k + L, cols]
        inv_cnt = 1.0 / jnp.minimum(pos, win).astype(F32)
        pooled = (s * inv_cnt - cur).reshape(R, POOL_GROUP_DIM).astype(BF16)
        ya.append(jnp.dot(pooled, pw_ref[g], preferred_element_type=F32) * ps_ref[:, cols])
    ya = jnp.concatenate(ya, axis=-1).astype(BF16)

    vb = v_ref[...].astype(BF16)
    yb = []
    for g in range(SGU_GROUPS):
        cols = slice(g * SGU_GROUP_DIM, (g + 1) * SGU_GROUP_DIM)
        s = jnp.dot(bd_ref[g], vb[:, cols], preferred_element_type=F32) + sb_ref[:, cols]
        yb.append(u_ref[:, cols].astype(F32) * s)
    yb = jnp.concatenate(yb, axis=-1).astype(BF16)

    conv = q_scr[:, CONV_PAD - 2:CONV_PAD - 2 + L, :] * cw_ref[0:1, :]
    for k in range(1, CONV_WIDTH):
        o = CONV_PAD - 2 + k
        conv = conv + q_scr[:, o:o + L, :] * cw_ref[k:k + 1, :]
    yc = (cb_ref[...].astype(F32) * conv.reshape(R, CONV_DIM)).astype(BF16)

    merged = ga_ref[...].astype(F32) * jnp.dot(ya, wa_ref[...], preferred_element_type=F32)
    merged = merged + gb_ref[...].astype(F32) * jnp.dot(yb, wb_ref[...], preferred_element_type=F32)
    merged = merged + gc_ref[...].astype(F32) * jnp.dot(yc, wc_ref[...], preferred_element_type=F32)
    o = jnp.dot(merged.astype(BF16), wo_ref[...], preferred_element_type=F32)
    x1 = _layer_norm(ALPHA * x_ref[...] + o, lg_ref[...], lb_ref[...])
    x1_ref[...] = x1
    x1b_ref[...] = x1.astype(BF16)

    if has_hist:
        np_ref[...] = p_scr[:, L:L + POOL_PAD, :]
        nc_ref[...] = q_scr[:, L:L + CONV_PAD, :]
    else:
        @pl.when(t == tiles_per_seq - 1)
        def _():
            np_ref[...] = p_scr[:, L:L + POOL_PAD, :]
            nc_ref[...] = q_scr[:, L:L + CONV_PAD, :]


def _mixer(x, acts, lw, sgu_bd, sgu_bias, *, nseq, seq_len, pos0, tiles_per_seq, hist=None):
    a, cb, cv, v, u, gates = acts
    n = x.shape[0]
    R = nseq * seq_len
    assert R == TM_MIX and n % R == 0
    n_tiles = n // R
    row = lambda i: (i, 0)
    has_hist = hist is not None
    n_state_seq = n_tiles * nseq // tiles_per_seq
    state_idx = lambda i: (i // tiles_per_seq, 0, 0)

    in_specs = [
        pl.BlockSpec((R, D_MODEL), row),
        pl.BlockSpec((R, POOL_DIM), row),
        pl.BlockSpec((R, CONV_DIM), row),
        pl.BlockSpec((R, CONV_DIM), row),
        pl.BlockSpec((R, SGU_DIM), row),
        pl.BlockSpec((R, SGU_DIM), row),
        pl.BlockSpec((R, D_MODEL), lambda i: (i, 0)),
        pl.BlockSpec((R, D_MODEL), lambda i: (i, 1)),
        pl.BlockSpec((R, D_MODEL), lambda i: (i, 2)),
    ]
    args = [x, a, cb, cv, v, u, gates, gates, gates]
    if has_hist:
        hp, hc = hist
        in_specs += [pl.BlockSpec((nseq, POOL_PAD, POOL_DIM), state_idx),
                     pl.BlockSpec((nseq, CONV_PAD, CONV_DIM), state_idx)]
        args += [hp, hc]
    weights = [lw["pool_w"], lw["pool_scale"], sgu_bd, sgu_bias, lw["conv_w"],
               lw["w_br_a"], lw["w_br_b"], lw["w_br_c"], lw["w_o"], lw["ln1_g"], lw["ln1_b"]]
    in_specs += [_const_spec(w.shape) for w in weights]
    args += weights

    out_shape = (
        jax.ShapeDtypeStruct((n, D_MODEL), F32),
        jax.ShapeDtypeStruct((n, D_MODEL), BF16),
        jax.ShapeDtypeStruct((n_state_seq, POOL_PAD, POOL_DIM), F32),
        jax.ShapeDtypeStruct((n_state_seq, CONV_PAD, CONV_DIM), F32),
    )
    out_specs = (
        pl.BlockSpec((R, D_MODEL), row),
        pl.BlockSpec((R, D_MODEL), row),
        pl.BlockSpec((nseq, POOL_PAD, POOL_DIM), state_idx),
        pl.BlockSpec((nseq, CONV_PAD, CONV_DIM), state_idx),
    )
    return pl.pallas_call(
        functools.partial(_mixer_body, nseq, seq_len, pos0, tiles_per_seq, has_hist),
        grid=(n_tiles,),
        in_specs=in_specs,
        out_specs=out_specs,
        out_shape=out_shape,
        scratch_shapes=[pltpu.VMEM((nseq, POOL_PAD + seq_len, POOL_DIM), F32),
                        pltpu.VMEM((nseq, CONV_PAD + seq_len, CONV_DIM), F32)],
        compiler_params=pltpu.CompilerParams(
            dimension_semantics=("arbitrary",), vmem_limit_bytes=VMEM_LIMIT),
        name="mixer_sample" if has_hist else "mixer_prompt",
    )(*args)


def _ffn_body(x_ref, xb_ref, p_ref, wg_ref, wu_ref, wd_ref, wpg_ref, wpe_ref, lg_ref, lb_ref,
              o_ref, ob_ref):
    j = pl.program_id(1)

    @pl.when(j == 0)
    def _():
        gate = _sigmoid(jnp.dot(xb_ref[...], wpg_ref[...], preferred_element_type=F32))
        emb = jnp.dot(p_ref[...].astype(BF16), wpe_ref[...], preferred_element_type=F32)
        o_ref[...] = ALPHA * x_ref[...] + gate * emb

    def swiglu_down(rows):
        xb = xb_ref[rows, :]
        hg = jnp.dot(xb, wg_ref[...], preferred_element_type=F32)
        hu = jnp.dot(xb, wu_ref[...], preferred_element_type=F32)
        act = (hg * _sigmoid(hg) * hu).astype(BF16)
        return jnp.dot(act, wd_ref[...], preferred_element_type=F32)

    last = pl.num_programs(1) - 1
    tm = x_ref.shape[0]

    @pl.when(j != last)
    def _():
        o_ref[...] += swiglu_down(slice(None))

    @pl.when(j == last)
    def _():
        mc = min(MC_LN, tm)
        for m in range(0, tm, mc):
            rows = pl.ds(m, mc)
            y = _layer_norm(o_ref[rows, :] + swiglu_down(rows), lg_ref[...], lb_ref[...])
            o_ref[rows, :] = y
            ob_ref[rows, :] = y.astype(BF16)


def _ffn(x1, x1b, p_all, p_row0, lw, *, tm):
    n = x1.shape[0]
    assert n % tm == 0 and p_row0 % tm == 0
    nj = D_FF // TF_FFN
    p_off = p_row0 // tm
    row = lambda i, j: (i, 0)
    return pl.pallas_call(
        _ffn_body,
        grid=(n // tm, nj),
        in_specs=[
            pl.BlockSpec((tm, D_MODEL), row),
            pl.BlockSpec((tm, D_MODEL), row),
            pl.BlockSpec((tm, PLE_DIM), lambda i, j: (i + p_off, 0)),
            pl.BlockSpec((D_MODEL, TF_FFN), lambda i, j: (0, j)),
            pl.BlockSpec((D_MODEL, TF_FFN), lambda i, j: (0, j + nj)),
            pl.BlockSpec((TF_FFN, D_MODEL), lambda i, j: (j, 0)),
            _const_spec((D_MODEL, D_MODEL)),
            _const_spec((PLE_DIM, D_MODEL)),
            _const_spec((1, D_MODEL)),
            _const_spec((1, D_MODEL)),
        ],
        out_specs=(pl.BlockSpec((tm, D_MODEL), row), pl.BlockSpec((tm, D_MODEL), row)),
        out_shape=(jax.ShapeDtypeStruct((n, D_MODEL), F32), jax.ShapeDtypeStruct((n, D_MODEL), BF16)),
        compiler_params=pltpu.CompilerParams(
            dimension_semantics=("arbitrary", "arbitrary"), vmem_limit_bytes=VMEM_LIMIT),
        name="ffn",
    )(x1, x1b, p_all, lw["w_gu"], lw["w_gu"], lw["w_down"], lw["w_pe_gate"], lw["w_pe"],
      lw["ln2_g"], lw["ln2_b"])


def _sgu_block_diag(sgu_w, sgu_b, chunk, rows):
    w = jnp.tril(sgu_w[:, :chunk, :chunk])
    reps = rows // chunk
    eye = jnp.eye(reps, dtype=w.dtype)
    bd = jnp.einsum("ab,gts->gatbs", eye, w).reshape(SGU_GROUPS, rows, rows)
    bias = jnp.tile(jnp.transpose(sgu_b[:, :chunk]), (reps, 1))
    bias = jnp.repeat(bias, SGU_GROUP_DIM, axis=1)
    return bd.astype(BF16), bias.astype(F32)


def _layer_weights(i, w_in, pool_w, pool_scale, sgu_ln_g, sgu_ln_b, sgu_w, sgu_b, conv_w, w_br_a,
                   w_br_b, w_br_c, w_o, ln1_g, ln1_b, w_gu, w_down, w_pe, w_pe_gate, ln2_g, ln2_b,
                   sample_len):
    off_u, off_v = POOL_DIM, POOL_DIM + SGU_DIM
    off_cb = off_v + SGU_DIM
    off_g = off_cb + 3 * CONV_DIM
    wi = w_in[i]
    w_in_p = jnp.concatenate(
        [wi[:, off_u:off_v], wi[:, off_v:off_cb], wi[:, :POOL_DIM], wi[:, off_cb:off_g],
         wi[:, off_g:]], axis=1).astype(BF16)
    row = lambda a: a[i].reshape(1, -1).astype(F32)
    return dict(
        w_in=w_in_p, sgu_ln_g=row(sgu_ln_g), sgu_ln_b=row(sgu_ln_b),
        pool_w=pool_w[i].astype(BF16), pool_scale=row(pool_scale),
        sgu_prompt=_sgu_block_diag(sgu_w[i], sgu_b[i], SGU_CHUNK, TM_MIX),
        sgu_sample=_sgu_block_diag(sgu_w[i], sgu_b[i], sample_len, TM_MIX),
        conv_w=conv_w[i].astype(F32),
        w_br_a=w_br_a[i].astype(BF16), w_br_b=w_br_b[i].astype(BF16), w_br_c=w_br_c[i].astype(BF16),
        w_o=w_o[i].astype(BF16), ln1_g=row(ln1_g), ln1_b=row(ln1_b),
        w_gu=w_gu[i].astype(BF16), w_down=w_down[i].astype(BF16),
        w_pe=w_pe[i].astype(BF16), w_pe_gate=w_pe_gate[i].astype(BF16),
        ln2_g=row(ln2_g), ln2_b=row(ln2_b),
    )


def kernel(x_prompt, x_sample, state_pool, state_conv, p_prompt, p_sample, w_in, pool_w, pool_scale,
           sgu_ln_g, sgu_ln_b, sgu_w, sgu_b, conv_w, w_br_a, w_br_b, w_br_c, w_o, ln1_g, ln1_b, w_gu,
           w_down, w_pe, w_pe_gate, ln2_g, ln2_b):
    batch, seq, _ = x_prompt.shape
    dec_batch, dec_seq, _ = x_sample.shape
    n_p = batch * seq
    n_s = dec_batch * dec_seq
    tm_in_p = 1024
    assert seq % TM_MIX == 0 and TM_MIX % dec_seq == 0 and n_s % TM_MIX == 0
    assert n_p % tm_in_p == 0 and n_p % TM_FFN == 0

    xp = x_prompt.reshape(n_p, D_MODEL)
    xs = x_sample.reshape(n_s, D_MODEL)
    xpb, xsb = xp, xs
    pp_all = p_prompt.reshape(DEPTH * n_p, PLE_DIM)
    ps_all = p_sample.reshape(DEPTH * n_s, PLE_DIM)
    hist_pool = jnp.pad(state_pool, ((0, 0), (0, 0), (POOL_PAD - POOL_HIST, 0), (0, 0)))
    hist_conv = jnp.pad(state_conv, ((0, 0), (0, 0), (CONV_PAD - (CONV_WIDTH - 1), 0), (0, 0)))

    outs = {k: [] for k in ("pool_p", "conv_p", "pool_s", "conv_s", "v_s")}
    for i in range(DEPTH):
        lw = _layer_weights(i, w_in, pool_w, pool_scale, sgu_ln_g, sgu_ln_b, sgu_w, sgu_b, conv_w,
                            w_br_a, w_br_b, w_br_c, w_o, ln1_g, ln1_b, w_gu, w_down, w_pe,
                            w_pe_gate, ln2_g, ln2_b, dec_seq)
        acts = _inproj(xpb, lw["w_in"], lw["sgu_ln_g"], lw["sgu_ln_b"], tm=tm_in_p, v_dtype=BF16)
        x1, x1b, pool_p, conv_p = _mixer(
            xp, acts, lw, *lw["sgu_prompt"], nseq=1, seq_len=TM_MIX, pos0=0,
            tiles_per_seq=seq // TM_MIX)
        xp, xpb = _ffn(x1, x1b, pp_all, i * n_p, lw, tm=TM_FFN)
        acts = _inproj(xsb, lw["w_in"], lw["sgu_ln_g"], lw["sgu_ln_b"], tm=n_s, v_dtype=F32)
        x1, x1b, pool_s, conv_s = _mixer(
            xs, acts, lw, *lw["sgu_sample"], nseq=TM_MIX // dec_seq, seq_len=dec_seq, pos0=1024,
            tiles_per_seq=1, hist=(hist_pool[i], hist_conv[i]))
        xs, xsb = _ffn(x1, x1b, ps_all, i * n_s, lw, tm=n_s)

        outs["pool_p"].append(pool_p[:, POOL_PAD - POOL_HIST:])
        outs["conv_p"].append(conv_p[:, CONV_PAD - (CONV_WIDTH - 1):])
        outs["pool_s"].append(pool_s[:, POOL_PAD - POOL_HIST:])
        outs["conv_s"].append(conv_s[:, CONV_PAD - (CONV_WIDTH - 1):])
        outs["v_s"].append(acts[3].reshape(dec_batch, dec_seq, SGU_DIM))

    return (xp.reshape(batch, seq, D_MODEL), xs.reshape(dec_batch, dec_seq, D_MODEL),
            jnp.stack(outs["pool_p"]), jnp.stack(outs["conv_p"]), jnp.stack(outs["pool_s"]),
            jnp.stack(outs["conv_s"]), jnp.stack(outs["v_s"]))
```

```python
import functools

import jax
import jax.numpy as jnp
from jax import lax
from jax.experimental import pallas as pl
from jax.experimental.pallas import tpu as pltpu

F32 = jnp.float32
BF16 = jnp.bfloat16

D_MODEL = 2048
DEPTH = 2
POOL_DIM = 512
POOL_WINDOWS = (2, 4, 8, 16)
POOL_GROUP_DIM = 128
POOL_HIST = 15
POOL_PAD = 16
SGU_DIM = 1024
SGU_GROUPS = 8
SGU_GROUP_DIM = 128
SGU_CHUNK = 128
CONV_DIM = 512
CONV_WIDTH = 3
CONV_PAD = 8
D_FF = 5632
PLE_DIM = 256
ALPHA = (2 * DEPTH) ** 0.25
LN_EPS = 1e-5
IN_COLS = 10240
GATE_COLS = 3 * D_MODEL

TN_IN = 1024
NC_IN = 256
MC_IN = 512
MC_LN = 256
TM_MIX = 256
OUT_CHUNK = 256
TM_FFN = 512
TF_FFN = 512
VMEM_LIMIT = 56 * 1024 * 1024


def _layer_norm(x, g, b):
    mu = jnp.mean(x, axis=-1, keepdims=True)
    xc = x - mu
    var = jnp.mean(xc * xc, axis=-1, keepdims=True)
    return xc * lax.rsqrt(var + LN_EPS) * g + b


def _gelu(x):
    return 0.5 * x * (1.0 + lax.erf(x * (2.0 ** -0.5)))


def _sigmoid(x):
    return 0.5 * jnp.tanh(0.5 * x) + 0.5


def _const_spec(shape):
    nd = len(shape)
    return pl.BlockSpec(shape, lambda *_: (0,) * nd, pipeline_mode=pl.Buffered(1))


def _inproj_body(x_ref, w_ref, lng_ref, lnb_ref,
                 a_ref, cb_ref, conv_ref, v_ref, u_ref, g_ref, *scratch):
    xb_ref = scratch[0] if scratch else x_ref
    j = pl.program_id(1)
    tm = x_ref.shape[0]
    mc = min(MC_IN, tm)
    nchunks = TN_IN // NC_IN
    half = nchunks // 2
    row_chunks = [pl.ds(m, mc) for m in range(0, tm, mc)]
    ln_chunks = [pl.ds(m, min(MC_LN, tm)) for m in range(0, tm, min(MC_LN, tm))]

    def cols(n):
        return slice(n * NC_IN, (n + 1) * NC_IN)

    def z(rows, n):
        return jnp.dot(xb_ref[rows, :], w_ref[:, cols(n)], preferred_element_type=F32)

    @pl.when(j == 0)
    def _():
        for rows in row_chunks:
            if scratch:
                xb_ref[rows, :] = x_ref[rows, :].astype(BF16)
            for n in range(nchunks):
                u_ref[rows, cols(n)] = _gelu(z(rows, n)).astype(u_ref.dtype)

    @pl.when(j == 1)
    def _():
        for rows in ln_chunks:
            v = jnp.concatenate([_gelu(z(rows, n)) for n in range(nchunks)], axis=-1)
            v_ref[rows, :] = _layer_norm(v, lng_ref[...], lnb_ref[...]).astype(v_ref.dtype)

    @pl.when(j == 2)
    def _():
        for rows in row_chunks:
            for n in range(half):
                a_ref[rows, cols(n)] = z(rows, n)
            for n in range(half):
                cb_ref[rows, cols(n)] = z(rows, half + n).astype(cb_ref.dtype)

    @pl.when(j == 3)
    def _():
        for rows in row_chunks:
            for n in range(half):
                conv_ref[rows, cols(n)] = z(rows, n) * z(rows, half + n)

    @pl.when(j >= 4)
    def _():
        for rows in row_chunks:
            for n in range(nchunks):
                g_ref[rows, cols(n)] = _sigmoid(z(rows, n)).astype(g_ref.dtype)


def _inproj(x, w_in_p, ln_g, ln_b, *, tm, v_dtype):
    n = x.shape[0]
    assert n % tm == 0
    n_tiles = n // tm
    nj = IN_COLS // TN_IN

    def early(k):
        return lambda i, j: (jnp.minimum(i + (j > k).astype(jnp.int32), n_tiles - 1), 0)

    out_shape = (
        jax.ShapeDtypeStruct((n, POOL_DIM), F32),
        jax.ShapeDtypeStruct((n, CONV_DIM), BF16),
        jax.ShapeDtypeStruct((n, CONV_DIM), F32),
        jax.ShapeDtypeStruct((n, SGU_DIM), v_dtype),
        jax.ShapeDtypeStruct((n, SGU_DIM), BF16),
        jax.ShapeDtypeStruct((n, GATE_COLS), BF16),
    )
    return pl.pallas_call(
        _inproj_body,
        grid=(n_tiles, nj),
        in_specs=[
            pl.BlockSpec((tm, D_MODEL), lambda i, j: (i, 0)),
            pl.BlockSpec((D_MODEL, TN_IN), lambda i, j: (0, j)),
            _const_spec((1, SGU_DIM)),
            _const_spec((1, SGU_DIM)),
        ],
        out_specs=(
            pl.BlockSpec((tm, POOL_DIM), early(2)),
            pl.BlockSpec((tm, CONV_DIM), early(2)),
            pl.BlockSpec((tm, CONV_DIM), early(3)),
            pl.BlockSpec((tm, SGU_DIM), early(1)),
            pl.BlockSpec((tm, SGU_DIM), early(0)),
            pl.BlockSpec((tm, TN_IN), lambda i, j: (i, jnp.maximum(j - 4, 0))),
        ),
        out_shape=out_shape,
        scratch_shapes=[] if x.dtype == BF16 else [pltpu.VMEM((tm, D_MODEL), BF16)],
        compiler_params=pltpu.CompilerParams(
            dimension_semantics=("arbitrary", "arbitrary"), vmem_limit_bytes=VMEM_LIMIT),
        name="inproj",
    )(x, w_in_p, ln_g, ln_b)


def _mixer_body(nseq, seq_len, pos0, tiles_per_seq, n_tiles, has_hist, *refs):
    if has_hist:
        (x_ref, a_ref, cb_ref, cv_ref, v_ref, u_ref, ga_ref, gb_ref, gc_ref, hp_ref, hc_ref,
         pw_ref, ps_ref, bd_ref, sb_ref, cw_ref, wa_ref, wb_ref, wc_ref, wo_ref, lg_ref, lb_ref,
         x1_ref, x1b_ref, np_ref, nc_ref,
         p_scr, q_scr, y0_scr, y1_scr, o0_scr, o1_scr, mb_scr) = refs
    else:
        (x_ref, a_ref, cb_ref, cv_ref, v_ref, u_ref, ga_ref, gb_ref, gc_ref,
         pw_ref, ps_ref, bd_ref, sb_ref, cw_ref, wa_ref, wb_ref, wc_ref, wo_ref, lg_ref, lb_ref,
         x1_ref, x1b_ref, np_ref, nc_ref,
         p_scr, q_scr, y0_scr, y1_scr, o0_scr, o1_scr, mb_scr) = refs
    L = seq_len
    R = nseq * L
    s = pl.program_id(0)
    t = jnp.minimum(s, n_tiles - 1) % tiles_per_seq
    off_b = POOL_DIM
    off_c = POOL_DIM + SGU_DIM

    @pl.when(s == 0)
    def _():
        y1_scr[...] = jnp.zeros(y1_scr.shape, y1_scr.dtype)
        o0_scr[...] = jnp.zeros(o0_scr.shape, o0_scr.dtype)
        o1_scr[...] = jnp.zeros(o1_scr.shape, o1_scr.dtype)

    if has_hist:
        p_scr[:, 0:POOL_PAD, :] = hp_ref[...]
        q_scr[:, 0:CONV_PAD, :] = hc_ref[...]
    else:
        @pl.when(t == 0)
        def _():
            p_scr[:, 0:POOL_PAD, :] = jnp.zeros((nseq, POOL_PAD, POOL_DIM), F32)
            q_scr[:, 0:CONV_PAD, :] = jnp.zeros((nseq, CONV_PAD, CONV_DIM), F32)

        @pl.when(t != 0)
        def _():
            p_scr[:, 0:POOL_PAD, :] = p_scr[:, L:L + POOL_PAD, :]
            q_scr[:, 0:CONV_PAD, :] = q_scr[:, L:L + CONV_PAD, :]

    def stages(y_head, y_proj, o_proj, o_norm):
        def pool(g):
            win = POOL_WINDOWS[g]
            cols = slice(g * POOL_GROUP_DIM, (g + 1) * POOL_GROUP_DIM)
            pos = pos0 + t * L + lax.broadcasted_iota(jnp.int32, (1, L, 1), 1) + 1
            cur = p_scr[:, POOL_PAD:POOL_PAD + L, cols]
            acc = cur
            for k in range(1, win):
                acc = acc + p_scr[:, POOL_PAD - k:POOL_PAD - k + L, cols]
            inv_cnt = 1.0 / jnp.minimum(pos, win).astype(F32)
            pooled = (acc * inv_cnt - cur).reshape(R, POOL_GROUP_DIM).astype(BF16)
            ya = jnp.dot(pooled, pw_ref[g], preferred_element_type=F32) * ps_ref[:, cols]
            y_head[:, cols] = ya.astype(BF16)

        def gating(g):
            cols = slice(g * SGU_GROUP_DIM, (g + 1) * SGU_GROUP_DIM)
            sg = jnp.dot(bd_ref[g], v_ref[:, cols].astype(BF16), preferred_element_type=F32)
            yb = u_ref[:, cols].astype(F32) * (sg + sb_ref[:, cols])
            y_head[:, off_b + g * SGU_GROUP_DIM:off_b + (g + 1) * SGU_GROUP_DIM] = yb.astype(BF16)

        def conv(c):
            cols = slice(c * POOL_GROUP_DIM, (c + 1) * POOL_GROUP_DIM)
            acc = q_scr[:, CONV_PAD - 2:CONV_PAD - 2 + L, cols] * cw_ref[0:1, cols]
            for k in range(1, CONV_WIDTH):
                o = CONV_PAD - 2 + k
                acc = acc + q_scr[:, o:o + L, cols] * cw_ref[k:k + 1, cols]
            yc = cb_ref[:, cols].astype(F32) * acc.reshape(R, POOL_GROUP_DIM)
            y_head[:, off_c + c * POOL_GROUP_DIM:off_c + (c + 1) * POOL_GROUP_DIM] = yc.astype(BF16)

        def norm(q):
            rows = pl.ds(q * (R // 4), R // 4)
            x1 = _layer_norm(ALPHA * x_ref[rows, :] + o_norm[rows, :], lg_ref[...], lb_ref[...])
            x1_ref[rows, :] = x1
            x1b_ref[rows, :] = x1.astype(BF16)

        def out_proj(c):
            cols = slice(c * OUT_CHUNK, (c + 1) * OUT_CHUNK)
            o_proj[:, cols] = jnp.dot(mb_scr[...], wo_ref[:, cols], preferred_element_type=F32)

        p_scr[:, POOL_PAD:, :] = a_ref[...].reshape(nseq, L, POOL_DIM)
        q_scr[:, CONV_PAD:, :] = cv_ref[...].reshape(nseq, L, CONV_DIM)
        norm(0)
        o_proj[...] = ga_ref[...].astype(F32) * jnp.dot(y_proj[:, :off_b], wa_ref[...],
                                                        preferred_element_type=F32)
        pool(0)
        pool(1)
        norm(1)
        o_proj[...] += gb_ref[...].astype(F32) * jnp.dot(y_proj[:, off_b:off_c], wb_ref[...],
                                                         preferred_element_type=F32)
        pool(2)
        pool(3)
        norm(2)
        mb_scr[...] = (o_proj[...] + gc_ref[...].astype(F32) * jnp.dot(
            y_proj[:, off_c:], wc_ref[...], preferred_element_type=F32)).astype(BF16)
        for g in range(0, 4):
            gating(g)
        norm(3)
        out_proj(0)
        for g in range(4, SGU_GROUPS):
            gating(g)
        out_proj(1)
        for c in range(CONV_DIM // POOL_GROUP_DIM):
            conv(c)
        for c in range(2, D_MODEL // OUT_CHUNK):
            out_proj(c)

    @pl.when(s % 2 == 0)
    def _():
        stages(y0_scr, y1_scr, o1_scr, o0_scr)

    @pl.when(s % 2 == 1)
    def _():
        stages(y1_scr, y0_scr, o0_scr, o1_scr)

    @pl.when((t == tiles_per_seq - 1) & (s < n_tiles))
    def _():
        np_ref[...] = p_scr[:, L:L + POOL_PAD, :]
        nc_ref[...] = q_scr[:, L:L + CONV_PAD, :]


def _mixer(x, acts, lw, sgu_bd, sgu_bias, *, nseq, seq_len, pos0, tiles_per_seq, hist=None):
    a, cb, cv, v, u, gates = acts
    n = x.shape[0]
    R = nseq * seq_len
    assert R == TM_MIX and n % R == 0
    n_tiles = n // R
    has_hist = hist is not None
    n_state_seq = n_tiles * nseq // tiles_per_seq
    last = n_tiles - 1
    head = lambda s: (jnp.minimum(s, last), 0)
    proj = lambda c: (lambda s: (jnp.clip(s - 1, 0, last), c))
    norm = lambda s: (jnp.clip(s - 2, 0, last), 0)
    state_idx = lambda s: (jnp.minimum(s, last) // tiles_per_seq, 0, 0)

    in_specs = [
        pl.BlockSpec((R, D_MODEL), norm),
        pl.BlockSpec((R, POOL_DIM), head),
        pl.BlockSpec((R, CONV_DIM), head),
        pl.BlockSpec((R, CONV_DIM), head),
        pl.BlockSpec((R, SGU_DIM), head),
        pl.BlockSpec((R, SGU_DIM), head),
        pl.BlockSpec((R, D_MODEL), proj(0)),
        pl.BlockSpec((R, D_MODEL), proj(1)),
        pl.BlockSpec((R, D_MODEL), proj(2)),
    ]
    args = [x, a, cb, cv, v, u, gates, gates, gates]
    if has_hist:
        hp, hc = hist
        in_specs += [pl.BlockSpec((nseq, POOL_PAD, POOL_DIM), state_idx),
                     pl.BlockSpec((nseq, CONV_PAD, CONV_DIM), state_idx)]
        args += [hp, hc]
    weights = [lw["pool_w"], lw["pool_scale"], sgu_bd, sgu_bias, lw["conv_w"],
               lw["w_br_a"], lw["w_br_b"], lw["w_br_c"], lw["w_o"], lw["ln1_g"], lw["ln1_b"]]
    in_specs += [_const_spec(w.shape) for w in weights]
    args += weights

    out_shape = (
        jax.ShapeDtypeStruct((n, D_MODEL), F32),
        jax.ShapeDtypeStruct((n, D_MODEL), BF16),
        jax.ShapeDtypeStruct((n_state_seq, POOL_PAD, POOL_DIM), F32),
        jax.ShapeDtypeStruct((n_state_seq, CONV_PAD, CONV_DIM), F32),
    )
    out_specs = (
        pl.BlockSpec((R, D_MODEL), norm),
        pl.BlockSpec((R, D_MODEL), norm),
        pl.BlockSpec((nseq, POOL_PAD, POOL_DIM), state_idx),
        pl.BlockSpec((nseq, CONV_PAD, CONV_DIM), state_idx),
    )
    return pl.pallas_call(
        functools.partial(_mixer_body, nseq, seq_len, pos0, tiles_per_seq, n_tiles, has_hist),
        grid=(n_tiles + 2,),
        in_specs=in_specs,
        out_specs=out_specs,
        out_shape=out_shape,
        scratch_shapes=[pltpu.VMEM((nseq, POOL_PAD + seq_len, POOL_DIM), F32),
                        pltpu.VMEM((nseq, CONV_PAD + seq_len, CONV_DIM), F32),
                        pltpu.VMEM((R, D_MODEL), BF16), pltpu.VMEM((R, D_MODEL), BF16),
                        pltpu.VMEM((R, D_MODEL), F32), pltpu.VMEM((R, D_MODEL), F32),
                        pltpu.VMEM((R, D_MODEL), BF16)],
        compiler_params=pltpu.CompilerParams(
            dimension_semantics=("arbitrary",), vmem_limit_bytes=VMEM_LIMIT),
        name="mixer_sample" if has_hist else "mixer_prompt",
    )(*args)


def _ffn_body(x_ref, xb_ref, p_ref, wg_ref, wu_ref, wd_ref, wpg_ref, wpe_ref, lg_ref, lb_ref,
              o_ref, ob_ref):
    j = pl.program_id(1)

    @pl.when(j == 0)
    def _():
        gate = _sigmoid(jnp.dot(xb_ref[...], wpg_ref[...], preferred_element_type=F32))
        emb = jnp.dot(p_ref[...].astype(BF16), wpe_ref[...], preferred_element_type=F32)
        o_ref[...] = ALPHA * x_ref[...] + gate * emb

    def swiglu_down(rows):
        xb = xb_ref[rows, :]
        hg = jnp.dot(xb, wg_ref[...], preferred_element_type=F32)
        hu = jnp.dot(xb, wu_ref[...], preferred_element_type=F32)
        act = (hg * _sigmoid(hg) * hu).astype(BF16)
        return jnp.dot(act, wd_ref[...], preferred_element_type=F32)

    last = pl.num_programs(1) - 1
    tm = x_ref.shape[0]

    @pl.when(j != last)
    def _():
        o_ref[...] += swiglu_down(slice(None))

    @pl.when(j == last)
    def _():
        mc = min(MC_LN, tm)
        for m in range(0, tm, mc):
            rows = pl.ds(m, mc)
            y = _layer_norm(o_ref[rows, :] + swiglu_down(rows), lg_ref[...], lb_ref[...])
            o_ref[rows, :] = y
            ob_ref[rows, :] = y.astype(BF16)


def _ffn(x1, x1b, p_all, p_row0, lw, *, tm):
    n = x1.shape[0]
    assert n % tm == 0 and p_row0 % tm == 0
    nj = D_FF // TF_FFN
    p_off = p_row0 // tm
    row = lambda i, j: (i, 0)
    return pl.pallas_call(
        _ffn_body,
        grid=(n // tm, nj),
        in_specs=[
            pl.BlockSpec((tm, D_MODEL), row),
            pl.BlockSpec((tm, D_MODEL), row),
            pl.BlockSpec((tm, PLE_DIM), lambda i, j: (i + p_off, 0)),
            pl.BlockSpec((D_MODEL, TF_FFN), lambda i, j: (0, j)),
            pl.BlockSpec((D_MODEL, TF_FFN), lambda i, j: (0, j + nj)),
            pl.BlockSpec((TF_FFN, D_MODEL), lambda i, j: (j, 0)),
            _const_spec((D_MODEL, D_MODEL)),
            _const_spec((PLE_DIM, D_MODEL)),
            _const_spec((1, D_MODEL)),
            _const_spec((1, D_MODEL)),
        ],
        out_specs=(pl.BlockSpec((tm, D_MODEL), row), pl.BlockSpec((tm, D_MODEL), row)),
        out_shape=(jax.ShapeDtypeStruct((n, D_MODEL), F32), jax.ShapeDtypeStruct((n, D_MODEL), BF16)),
        compiler_params=pltpu.CompilerParams(
            dimension_semantics=("arbitrary", "arbitrary"), vmem_limit_bytes=VMEM_LIMIT),
        name="ffn",
    )(x1, x1b, p_all, lw["w_gu"], lw["w_gu"], lw["w_down"], lw["w_pe_gate"], lw["w_pe"],
      lw["ln2_g"], lw["ln2_b"])


def _sgu_block_diag(sgu_w, sgu_b, chunk, rows):
    w = jnp.tril(sgu_w[:, :chunk, :chunk])
    reps = rows // chunk
    eye = jnp.eye(reps, dtype=w.dtype)
    bd = jnp.einsum("ab,gts->gatbs", eye, w).reshape(SGU_GROUPS, rows, rows)
    bias = jnp.tile(jnp.transpose(sgu_b[:, :chunk]), (reps, 1))
    bias = jnp.repeat(bias, SGU_GROUP_DIM, axis=1)
    return bd.astype(BF16), bias.astype(F32)


def _layer_weights(i, w_in, pool_w, pool_scale, sgu_ln_g, sgu_ln_b, sgu_w, sgu_b, conv_w, w_br_a,
                   w_br_b, w_br_c, w_o, ln1_g, ln1_b, w_gu, w_down, w_pe, w_pe_gate, ln2_g, ln2_b,
                   sample_len):
    off_u, off_v = POOL_DIM, POOL_DIM + SGU_DIM
    off_cb = off_v + SGU_DIM
    off_g = off_cb + 3 * CONV_DIM
    wi = w_in[i]
    w_in_p = jnp.concatenate(
        [wi[:, off_u:off_v], wi[:, off_v:off_cb], wi[:, :POOL_DIM], wi[:, off_cb:off_g],
         wi[:, off_g:]], axis=1).astype(BF16)
    row = lambda a: a[i].reshape(1, -1).astype(F32)
    return dict(
        w_in=w_in_p, sgu_ln_g=row(sgu_ln_g), sgu_ln_b=row(sgu_ln_b),
        pool_w=pool_w[i].astype(BF16), pool_scale=row(pool_scale),
        sgu_prompt=_sgu_block_diag(sgu_w[i], sgu_b[i], SGU_CHUNK, TM_MIX),
        sgu_sample=_sgu_block_diag(sgu_w[i], sgu_b[i], sample_len, TM_MIX),
        conv_w=conv_w[i].astype(F32),
        w_br_a=w_br_a[i].astype(BF16), w_br_b=w_br_b[i].astype(BF16), w_br_c=w_br_c[i].astype(BF16),
        w_o=w_o[i].astype(BF16), ln1_g=row(ln1_g), ln1_b=row(ln1_b),
        w_gu=w_gu[i].astype(BF16), w_down=w_down[i].astype(BF16),
        w_pe=w_pe[i].astype(BF16), w_pe_gate=w_pe_gate[i].astype(BF16),
        ln2_g=row(ln2_g), ln2_b=row(ln2_b),
    )


def kernel(x_prompt, x_sample, state_pool, state_conv, p_prompt, p_sample, w_in, pool_w, pool_scale,
           sgu_ln_g, sgu_ln_b, sgu_w, sgu_b, conv_w, w_br_a, w_br_b, w_br_c, w_o, ln1_g, ln1_b, w_gu,
           w_down, w_pe, w_pe_gate, ln2_g, ln2_b):
    batch, seq, _ = x_prompt.shape
    dec_batch, dec_seq, _ = x_sample.shape
    n_p = batch * seq
    n_s = dec_batch * dec_seq
    tm_in_p = 1024
    assert seq % TM_MIX == 0 and TM_MIX % dec_seq == 0 and n_s % TM_MIX == 0
    assert n_p % tm_in_p == 0 and n_p % TM_FFN == 0

    xp = x_prompt.reshape(n_p, D_MODEL)
    xs = x_sample.reshape(n_s, D_MODEL)
    xpb, xsb = xp, xs
    pp_all = p_prompt.reshape(DEPTH * n_p, PLE_DIM)
    ps_all = p_sample.reshape(DEPTH * n_s, PLE_DIM)
    hist_pool = jnp.pad(state_pool, ((0, 0), (0, 0), (POOL_PAD - POOL_HIST, 0), (0, 0)))
    hist_conv = jnp.pad(state_conv, ((0, 0), (0, 0), (CONV_PAD - (CONV_WIDTH - 1), 0), (0, 0)))

    outs = {k: [] for k in ("pool_p", "conv_p", "pool_s", "conv_s", "v_s")}
    for i in range(DEPTH):
        lw = _layer_weights(i, w_in, pool_w, pool_scale, sgu_ln_g, sgu_ln_b, sgu_w, sgu_b, conv_w,
                            w_br_a, w_br_b, w_br_c, w_o, ln1_g, ln1_b, w_gu, w_down, w_pe,
                            w_pe_gate, ln2_g, ln2_b, dec_seq)
        acts = _inproj(xpb, lw["w_in"], lw["sgu_ln_g"], lw["sgu_ln_b"], tm=tm_in_p, v_dtype=BF16)
        x1, x1b, pool_p, conv_p = _mixer(
            xp, acts, lw, *lw["sgu_prompt"], nseq=1, seq_len=TM_MIX, pos0=0,
            tiles_per_seq=seq // TM_MIX)
        xp, xpb = _ffn(x1, x1b, pp_all, i * n_p, lw, tm=TM_FFN)
        acts = _inproj(xsb, lw["w_in"], lw["sgu_ln_g"], lw["sgu_ln_b"], tm=n_s, v_dtype=F32)
        x1, x1b, pool_s, conv_s = _mixer(
            xs, acts, lw, *lw["sgu_sample"], nseq=TM_MIX // dec_seq, seq_len=dec_seq, pos0=1024,
            tiles_per_seq=1, hist=(hist_pool[i], hist_conv[i]))
        xs, xsb = _ffn(x1, x1b, ps_all, i * n_s, lw, tm=n_s)

        outs["pool_p"].append(pool_p[:, POOL_PAD - POOL_HIST:])
        outs["conv_p"].append(conv_p[:, CONV_PAD - (CONV_WIDTH - 1):])
        outs["pool_s"].append(pool_s[:, POOL_PAD - POOL_HIST:])
        outs["conv_s"].append(conv_s[:, CONV_PAD - (CONV_WIDTH - 1):])
        outs["v_s"].append(acts[3].reshape(dec_batch, dec_seq, SGU_DIM))

    return (xp.reshape(batch, seq, D_MODEL), xs.reshape(dec_batch, dec_seq, D_MODEL),
            jnp.stack(outs["pool_p"]), jnp.stack(outs["conv_p"]), jnp.stack(outs["pool_s"]),
            jnp.stack(outs["conv_s"]), jnp.stack(outs["v_s"]))
```

```python
import functools

import jax
import jax.numpy as jnp
from jax import lax
from jax.experimental import pallas as pl
from jax.experimental.pallas import tpu as pltpu

F32 = jnp.float32
BF16 = jnp.bfloat16

D_MODEL = 2048
DEPTH = 2
POOL_DIM = 512
POOL_WINDOWS = (2, 4, 8, 16)
POOL_GROUP_DIM = 128
POOL_HIST = 15
POOL_PAD = 16
SGU_DIM = 1024
SGU_GROUPS = 8
SGU_GROUP_DIM = 128
SGU_CHUNK = 128
CONV_DIM = 512
CONV_WIDTH = 3
CONV_PAD = 8
D_FF = 5632
PLE_DIM = 256
ALPHA = (2 * DEPTH) ** 0.25
LN_EPS = 1e-5
IN_COLS = 10240
GATE_COLS = 3 * D_MODEL

TN_IN = 1024
NC_IN = 256
MC_IN = 512
MC_LN = 256
TM_MIX = 256
TM_FFN = 512
TF_FFN = 512
CAST_ROWS = 512
VMEM_LIMIT = 56 * 1024 * 1024


def _layer_norm(x, g, b):
    mu = jnp.mean(x, axis=-1, keepdims=True)
    xc = x - mu
    var = jnp.mean(xc * xc, axis=-1, keepdims=True)
    return xc * lax.rsqrt(var + LN_EPS) * g + b


def _gelu(x):
    return 0.5 * x * (1.0 + lax.erf(x * (2.0 ** -0.5)))


def _sigmoid(x):
    return 0.5 * jnp.tanh(0.5 * x) + 0.5


def _const_spec(shape):
    nd = len(shape)
    return pl.BlockSpec(shape, lambda *_: (0,) * nd, pipeline_mode=pl.Buffered(1))


def _inproj_body(x_ref, w_ref, lng_ref, lnb_ref,
                 a_ref, cb_ref, conv_ref, v_ref, u_ref, g_ref, *scratch):
    xb_ref = scratch[0] if scratch else x_ref
    j = pl.program_id(1)
    tm = x_ref.shape[0]
    mc = min(MC_IN, tm)
    nchunks = TN_IN // NC_IN
    half = nchunks // 2
    row_chunks = [pl.ds(m, mc) for m in range(0, tm, mc)]
    ln_chunks = [pl.ds(m, min(MC_LN, tm)) for m in range(0, tm, min(MC_LN, tm))]

    def cols(n):
        return slice(n * NC_IN, (n + 1) * NC_IN)

    def z(rows, n):
        return jnp.dot(xb_ref[rows, :], w_ref[:, cols(n)], preferred_element_type=F32)

    @pl.when(j == 0)
    def _():
        for rows in row_chunks:
            if scratch:
                xb_ref[rows, :] = x_ref[rows, :].astype(BF16)
            for n in range(nchunks):
                u_ref[rows, cols(n)] = _gelu(z(rows, n)).astype(u_ref.dtype)

    @pl.when(j == 1)
    def _():
        for rows in ln_chunks:
            v = jnp.concatenate([_gelu(z(rows, n)) for n in range(nchunks)], axis=-1)
            v_ref[rows, :] = _layer_norm(v, lng_ref[...], lnb_ref[...]).astype(v_ref.dtype)

    @pl.when(j == 2)
    def _():
        for rows in row_chunks:
            for n in range(half):
                a_ref[rows, cols(n)] = z(rows, n)
            for n in range(half):
                cb_ref[rows, cols(n)] = z(rows, half + n).astype(cb_ref.dtype)

    @pl.when(j == 3)
    def _():
        for rows in row_chunks:
            for n in range(half):
                conv_ref[rows, cols(n)] = z(rows, n) * z(rows, half + n)

    @pl.when(j >= 4)
    def _():
        for rows in row_chunks:
            for n in range(nchunks):
                g_ref[rows, cols(n)] = _sigmoid(z(rows, n)).astype(g_ref.dtype)


def _inproj(x, w_in_p, ln_g, ln_b, *, tm, v_dtype):
    n = x.shape[0]
    assert n % tm == 0
    n_tiles = n // tm
    nj = IN_COLS // TN_IN

    def early(k):
        return lambda i, j: (jnp.minimum(i + (j > k).astype(jnp.int32), n_tiles - 1), 0)

    out_shape = (
        jax.ShapeDtypeStruct((n, POOL_DIM), F32),
        jax.ShapeDtypeStruct((n, CONV_DIM), BF16),
        jax.ShapeDtypeStruct((n, CONV_DIM), F32),
        jax.ShapeDtypeStruct((n, SGU_DIM), v_dtype),
        jax.ShapeDtypeStruct((n, SGU_DIM), BF16),
        jax.ShapeDtypeStruct((n, GATE_COLS), BF16),
    )
    return pl.pallas_call(
        _inproj_body,
        grid=(n_tiles, nj),
        in_specs=[
            pl.BlockSpec((tm, D_MODEL), lambda i, j: (i, 0)),
            pl.BlockSpec((D_MODEL, TN_IN), lambda i, j: (0, j)),
            _const_spec((1, SGU_DIM)),
            _const_spec((1, SGU_DIM)),
        ],
        out_specs=(
            pl.BlockSpec((tm, POOL_DIM), early(2)),
            pl.BlockSpec((tm, CONV_DIM), early(2)),
            pl.BlockSpec((tm, CONV_DIM), early(3)),
            pl.BlockSpec((tm, SGU_DIM), early(1)),
            pl.BlockSpec((tm, SGU_DIM), early(0)),
            pl.BlockSpec((tm, TN_IN), lambda i, j: (i, jnp.maximum(j - 4, 0))),
        ),
        out_shape=out_shape,
        scratch_shapes=[] if x.dtype == BF16 else [pltpu.VMEM((tm, D_MODEL), BF16)],
        compiler_params=pltpu.CompilerParams(
            dimension_semantics=("arbitrary", "arbitrary"), vmem_limit_bytes=VMEM_LIMIT),
        name="inproj",
    )(x, w_in_p, ln_g, ln_b)


def _mixer_body(nseq, seq_len, pos0, tiles_per_seq, has_hist, *refs):
    if has_hist:
        (x_ref, a_ref, cb_ref, cv_ref, v_ref, u_ref, ga_ref, gb_ref, gc_ref, hp_ref, hc_ref,
         pw_ref, ps_ref, bd_ref, sb_ref, cw_ref, wa_ref, wb_ref, wc_ref, wo_ref, lg_ref, lb_ref,
         x1_ref, x1b_ref, np_ref, nc_ref, p_scr, q_scr) = refs
    else:
        (x_ref, a_ref, cb_ref, cv_ref, v_ref, u_ref, ga_ref, gb_ref, gc_ref,
         pw_ref, ps_ref, bd_ref, sb_ref, cw_ref, wa_ref, wb_ref, wc_ref, wo_ref, lg_ref, lb_ref,
         x1_ref, x1b_ref, np_ref, nc_ref, p_scr, q_scr) = refs
    L = seq_len
    R = nseq * L
    t = pl.program_id(0) % tiles_per_seq

    if has_hist:
        p_scr[:, 0:POOL_PAD, :] = hp_ref[...]
        q_scr[:, 0:CONV_PAD, :] = hc_ref[...]
    else:
        @pl.when(t == 0)
        def _():
            p_scr[:, 0:POOL_PAD, :] = jnp.zeros((nseq, POOL_PAD, POOL_DIM), F32)
            q_scr[:, 0:CONV_PAD, :] = jnp.zeros((nseq, CONV_PAD, CONV_DIM), F32)

        @pl.when(t != 0)
        def _():
            p_scr[:, 0:POOL_PAD, :] = p_scr[:, L:L + POOL_PAD, :]
            q_scr[:, 0:CONV_PAD, :] = q_scr[:, L:L + CONV_PAD, :]
    p_scr[:, POOL_PAD:, :] = a_ref[...].reshape(nseq, L, POOL_DIM)
    q_scr[:, CONV_PAD:, :] = cv_ref[...].reshape(nseq, L, CONV_DIM)

    pos = pos0 + t * L + lax.broadcasted_iota(jnp.int32, (1, L, 1), 1) + 1
    ya = []
    for g, win in enumerate(POOL_WINDOWS):
        cols = slice(g * POOL_GROUP_DIM, (g + 1) * POOL_GROUP_DIM)
        cur = p_scr[:, POOL_PAD:POOL_PAD + L, cols]
        s = cur
        for k in range(1, win):
            s = s + p_scr[:, POOL_PAD - k:POOL_PAD - k + L, cols]
        inv_cnt = 1.0 / jnp.minimum(pos, win).astype(F32)
        pooled = (s * inv_cnt - cur).reshape(R, POOL_GROUP_DIM).astype(BF16)
        ya.append(jnp.dot(pooled, pw_ref[g], preferred_element_type=F32) * ps_ref[:, cols])
    ya = jnp.concatenate(ya, axis=-1).astype(BF16)

    vb = v_ref[...].astype(BF16)
    yb = []
    for g in range(SGU_GROUPS):
        cols = slice(g * SGU_GROUP_DIM, (g + 1) * SGU_GROUP_DIM)
        s = jnp.dot(bd_ref[g], vb[:, cols], preferred_element_type=F32) + sb_ref[:, cols]
        yb.append(u_ref[:, cols].astype(F32) * s)
    yb = jnp.concatenate(yb, axis=-1).astype(BF16)

    conv = q_scr[:, CONV_PAD - 2:CONV_PAD - 2 + L, :] * cw_ref[0:1, :]
    for k in range(1, CONV_WIDTH):
        o = CONV_PAD - 2 + k
        conv = conv + q_scr[:, o:o + L, :] * cw_ref[k:k + 1, :]
    yc = (cb_ref[...].astype(F32) * conv.reshape(R, CONV_DIM)).astype(BF16)

    merged = ga_ref[...].astype(F32) * jnp.dot(ya, wa_ref[...], preferred_element_type=F32)
    merged = merged + gb_ref[...].astype(F32) * jnp.dot(yb, wb_ref[...], preferred_element_type=F32)
    merged = merged + gc_ref[...].astype(F32) * jnp.dot(yc, wc_ref[...], preferred_element_type=F32)
    o = jnp.dot(merged.astype(BF16), wo_ref[...], preferred_element_type=F32)
    x1 = _layer_norm(ALPHA * x_ref[...] + o, lg_ref[...], lb_ref[...])
    x1_ref[...] = x1
    x1b_ref[...] = x1.astype(BF16)

    if has_hist:
        np_ref[...] = p_scr[:, L:L + POOL_PAD, :]
        nc_ref[...] = q_scr[:, L:L + CONV_PAD, :]
    else:
        @pl.when(t == tiles_per_seq - 1)
        def _():
            np_ref[...] = p_scr[:, L:L + POOL_PAD, :]
            nc_ref[...] = q_scr[:, L:L + CONV_PAD, :]


def _mixer(x, acts, lw, sgu_bd, sgu_bias, *, nseq, seq_len, pos0, tiles_per_seq, hist=None):
    a, cb, cv, v, u, gates = acts
    n = x.shape[0]
    R = nseq * seq_len
    assert R == TM_MIX and n % R == 0
    n_tiles = n // R
    row = lambda i: (i, 0)
    has_hist = hist is not None
    n_state_seq = n_tiles * nseq // tiles_per_seq
    state_idx = lambda i: (i // tiles_per_seq, 0, 0)

    in_specs = [
        pl.BlockSpec((R, D_MODEL), row),
        pl.BlockSpec((R, POOL_DIM), row),
        pl.BlockSpec((R, CONV_DIM), row),
        pl.BlockSpec((R, CONV_DIM), row),
        pl.BlockSpec((R, SGU_DIM), row),
        pl.BlockSpec((R, SGU_DIM), row),
        pl.BlockSpec((R, D_MODEL), lambda i: (i, 0)),
        pl.BlockSpec((R, D_MODEL), lambda i: (i, 1)),
        pl.BlockSpec((R, D_MODEL), lambda i: (i, 2)),
    ]
    args = [x, a, cb, cv, v, u, gates, gates, gates]
    if has_hist:
        hp, hc = hist
        in_specs += [pl.BlockSpec((nseq, POOL_PAD, POOL_DIM), state_idx),
                     pl.BlockSpec((nseq, CONV_PAD, CONV_DIM), state_idx)]
        args += [hp, hc]
    weights = [lw["pool_w"], lw["pool_scale"], sgu_bd, sgu_bias, lw["conv_w"],
               lw["w_br_a"], lw["w_br_b"], lw["w_br_c"], lw["w_o"], lw["ln1_g"], lw["ln1_b"]]
    in_specs += [_const_spec(w.shape) for w in weights]
    args += weights

    out_shape = (
        jax.ShapeDtypeStruct((n, D_MODEL), F32),
        jax.ShapeDtypeStruct((n, D_MODEL), BF16),
        jax.ShapeDtypeStruct((n_state_seq, POOL_PAD, POOL_DIM), F32),
        jax.ShapeDtypeStruct((n_state_seq, CONV_PAD, CONV_DIM), F32),
    )
    out_specs = (
        pl.BlockSpec((R, D_MODEL), row),
        pl.BlockSpec((R, D_MODEL), row),
        pl.BlockSpec((nseq, POOL_PAD, POOL_DIM), state_idx),
        pl.BlockSpec((nseq, CONV_PAD, CONV_DIM), state_idx),
    )
    return pl.pallas_call(
        functools.partial(_mixer_body, nseq, seq_len, pos0, tiles_per_seq, has_hist),
        grid=(n_tiles,),
        in_specs=in_specs,
        out_specs=out_specs,
        out_shape=out_shape,
        scratch_shapes=[pltpu.VMEM((nseq, POOL_PAD + seq_len, POOL_DIM), F32),
                        pltpu.VMEM((nseq, CONV_PAD + seq_len, CONV_DIM), F32)],
        compiler_params=pltpu.CompilerParams(
            dimension_semantics=("arbitrary",), vmem_limit_bytes=VMEM_LIMIT),
        name="mixer_sample" if has_hist else "mixer_prompt",
    )(*args)


def _ffn_body(x_ref, xb_ref, p_ref, wg_ref, wu_ref, wd_ref, wpg_ref, wpe_ref, lg_ref, lb_ref,
              o_ref, ob_ref):
    j = pl.program_id(1)

    @pl.when(j == 0)
    def _():
        gate = _sigmoid(jnp.dot(xb_ref[...], wpg_ref[...], preferred_element_type=F32))
        emb = jnp.dot(p_ref[...].astype(BF16), wpe_ref[...], preferred_element_type=F32)
        o_ref[...] = ALPHA * x_ref[...] + gate * emb

    def swiglu_down(rows):
        xb = xb_ref[rows, :]
        hg = jnp.dot(xb, wg_ref[...], preferred_element_type=F32)
        hu = jnp.dot(xb, wu_ref[...], preferred_element_type=F32)
        act = (hg * _sigmoid(hg) * hu).astype(BF16)
        return jnp.dot(act, wd_ref[...], preferred_element_type=F32)

    last = pl.num_programs(1) - 1
    tm = x_ref.shape[0]

    @pl.when(j != last)
    def _():
        o_ref[...] += swiglu_down(slice(None))

    @pl.when(j == last)
    def _():
        mc = min(MC_LN, tm)
        for m in range(0, tm, mc):
            rows = pl.ds(m, mc)
            y = _layer_norm(o_ref[rows, :] + swiglu_down(rows), lg_ref[...], lb_ref[...])
            o_ref[rows, :] = y
            ob_ref[rows, :] = y.astype(BF16)


def _ffn(x1, x1b, p_all, p_row0, lw, *, tm):
    n = x1.shape[0]
    assert n % tm == 0 and p_row0 % tm == 0
    nj = D_FF // TF_FFN
    p_off = p_row0 // tm
    row = lambda i, j: (i, 0)
    return pl.pallas_call(
        _ffn_body,
        grid=(n // tm, nj),
        in_specs=[
            pl.BlockSpec((tm, D_MODEL), row),
            pl.BlockSpec((tm, D_MODEL), row),
            pl.BlockSpec((tm, PLE_DIM), lambda i, j: (i + p_off, 0)),
            pl.BlockSpec((D_MODEL, TF_FFN), lambda i, j: (0, j)),
            pl.BlockSpec((D_MODEL, TF_FFN), lambda i, j: (0, j + nj)),
            pl.BlockSpec((TF_FFN, D_MODEL), lambda i, j: (j, 0)),
            _const_spec((D_MODEL, D_MODEL)),
            _const_spec((PLE_DIM, D_MODEL)),
            _const_spec((1, D_MODEL)),
            _const_spec((1, D_MODEL)),
        ],
        out_specs=(pl.BlockSpec((tm, D_MODEL), row), pl.BlockSpec((tm, D_MODEL), row)),
        out_shape=(jax.ShapeDtypeStruct((n, D_MODEL), F32), jax.ShapeDtypeStruct((n, D_MODEL), BF16)),
        compiler_params=pltpu.CompilerParams(
            dimension_semantics=("arbitrary", "arbitrary"), vmem_limit_bytes=VMEM_LIMIT),
        name="ffn",
    )(x1, x1b, p_all, lw["w_gu"], lw["w_gu"], lw["w_down"], lw["w_pe_gate"], lw["w_pe"],
      lw["ln2_g"], lw["ln2_b"])


def _cast_body(w_ref, o_ref):
    o_ref[...] = w_ref[...].astype(o_ref.dtype)


def _cast_bf16(w, layer, block, src_col_block=None):
    _, rows, cols = w.shape
    br, bc = block
    assert rows % br == 0 and cols % bc == 0
    src = src_col_block if src_col_block is not None else (lambda j: j)
    return pl.pallas_call(
        _cast_body,
        grid=(rows // br, cols // bc),
        in_specs=[pl.BlockSpec((None, br, bc), lambda i, j: (layer, i, src(j)))],
        out_specs=pl.BlockSpec((br, bc), lambda i, j: (i, j)),
        out_shape=jax.ShapeDtypeStruct((rows, cols), BF16),
        compiler_params=pltpu.CompilerParams(
            dimension_semantics=("arbitrary", "arbitrary"), vmem_limit_bytes=VMEM_LIMIT),
        name="cast_bf16",
    )(w)


def _w_in_src_block(j):
    n_uv = 2 * SGU_DIM // POOL_DIM
    return jnp.where(j < n_uv, j + 1, jnp.where(j == n_uv, 0, j))


def _sgu_block_diag(sgu_w, sgu_b, chunk, rows):
    w = jnp.tril(sgu_w[:, :chunk, :chunk])
    reps = rows // chunk
    eye = jnp.eye(reps, dtype=w.dtype)
    bd = jnp.einsum("ab,gts->gatbs", eye, w).reshape(SGU_GROUPS, rows, rows)
    bias = jnp.tile(jnp.transpose(sgu_b[:, :chunk]), (reps, 1))
    bias = jnp.repeat(bias, SGU_GROUP_DIM, axis=1)
    return bd.astype(BF16), bias.astype(F32)


def _layer_weights(i, w_in, pool_w, pool_scale, sgu_ln_g, sgu_ln_b, sgu_w, sgu_b, conv_w, w_br_a,
                   w_br_b, w_br_c, w_o, ln1_g, ln1_b, w_gu, w_down, w_pe, w_pe_gate, ln2_g, ln2_b,
                   sample_len):
    wide = (CAST_ROWS, D_MODEL)
    row = lambda a: a[i].reshape(1, -1).astype(F32)
    return dict(
        w_in=_cast_bf16(w_in, i, (D_MODEL, POOL_DIM), _w_in_src_block),
        sgu_ln_g=row(sgu_ln_g), sgu_ln_b=row(sgu_ln_b),
        pool_w=pool_w[i].astype(BF16), pool_scale=row(pool_scale),
        sgu_prompt=_sgu_block_diag(sgu_w[i], sgu_b[i], SGU_CHUNK, TM_MIX),
        sgu_sample=_sgu_block_diag(sgu_w[i], sgu_b[i], sample_len, TM_MIX),
        conv_w=conv_w[i].astype(F32),
        w_br_a=_cast_bf16(w_br_a, i, wide), w_br_b=_cast_bf16(w_br_b, i, wide),
        w_br_c=_cast_bf16(w_br_c, i, wide), w_o=_cast_bf16(w_o, i, wide),
        ln1_g=row(ln1_g), ln1_b=row(ln1_b),
        w_gu=_cast_bf16(w_gu, i, (D_MODEL, TF_FFN)), w_down=_cast_bf16(w_down, i, wide),
        w_pe=_cast_bf16(w_pe, i, (PLE_DIM, D_MODEL)), w_pe_gate=_cast_bf16(w_pe_gate, i, wide),
        ln2_g=row(ln2_g), ln2_b=row(ln2_b),
    )


def kernel(x_prompt, x_sample, state_pool, state_conv, p_prompt, p_sample, w_in, pool_w, pool_scale,
           sgu_ln_g, sgu_ln_b, sgu_w, sgu_b, conv_w, w_br_a, w_br_b, w_br_c, w_o, ln1_g, ln1_b, w_gu,
           w_down, w_pe, w_pe_gate, ln2_g, ln2_b):
    batch, seq, _ = x_prompt.shape
    dec_batch, dec_seq, _ = x_sample.shape
    n_p = batch * seq
    n_s = dec_batch * dec_seq
    tm_in_p = 1024
    assert seq % TM_MIX == 0 and TM_MIX % dec_seq == 0 and n_s % TM_MIX == 0
    assert n_p % tm_in_p == 0 and n_p % TM_FFN == 0

    xp = x_prompt.reshape(n_p, D_MODEL)
    xs = x_sample.reshape(n_s, D_MODEL)
    xpb, xsb = xp, xs
    pp_all = p_prompt.reshape(DEPTH * n_p, PLE_DIM)
    ps_all = p_sample.reshape(DEPTH * n_s, PLE_DIM)
    hist_pool = jnp.pad(state_pool, ((0, 0), (0, 0), (POOL_PAD - POOL_HIST, 0), (0, 0)))
    hist_conv = jnp.pad(state_conv, ((0, 0), (0, 0), (CONV_PAD - (CONV_WIDTH - 1), 0), (0, 0)))

    outs = {k: [] for k in ("pool_p", "conv_p", "pool_s", "conv_s", "v_s")}
    for i in range(DEPTH):
        lw = _layer_weights(i, w_in, pool_w, pool_scale, sgu_ln_g, sgu_ln_b, sgu_w, sgu_b, conv_w,
                            w_br_a, w_br_b, w_br_c, w_o, ln1_g, ln1_b, w_gu, w_down, w_pe,
                            w_pe_gate, ln2_g, ln2_b, dec_seq)
        acts = _inproj(xpb, lw["w_in"], lw["sgu_ln_g"], lw["sgu_ln_b"], tm=tm_in_p, v_dtype=BF16)
        x1, x1b, pool_p, conv_p = _mixer(
            xp, acts, lw, *lw["sgu_prompt"], nseq=1, seq_len=TM_MIX, pos0=0,
            tiles_per_seq=seq // TM_MIX)
        xp, xpb = _ffn(x1, x1b, pp_all, i * n_p, lw, tm=TM_FFN)
        acts = _inproj(xsb, lw["w_in"], lw["sgu_ln_g"], lw["sgu_ln_b"], tm=n_s, v_dtype=F32)
        x1, x1b, pool_s, conv_s = _mixer(
            xs, acts, lw, *lw["sgu_sample"], nseq=TM_MIX // dec_seq, seq_len=dec_seq, pos0=1024,
            tiles_per_seq=1, hist=(hist_pool[i], hist_conv[i]))
        xs, xsb = _ffn(x1, x1b, ps_all, i * n_s, lw, tm=n_s)

        outs["pool_p"].append(pool_p[:, POOL_PAD - POOL_HIST:])
        outs["conv_p"].append(conv_p[:, CONV_PAD - (CONV_WIDTH - 1):])
        outs["pool_s"].append(pool_s[:, POOL_PAD - POOL_HIST:])
        outs["conv_s"].append(conv_s[:, CONV_PAD - (CONV_WIDTH - 1):])
        outs["v_s"].append(acts[3].reshape(dec_batch, dec_seq, SGU_DIM))

    return (xp.reshape(batch, seq, D_MODEL), xs.reshape(dec_batch, dec_seq, D_MODEL),
            jnp.stack(outs["pool_p"]), jnp.stack(outs["conv_p"]), jnp.stack(outs["pool_s"]),
            jnp.stack(outs["conv_s"]), jnp.stack(outs["v_s"]))
```

```python
import functools

import jax
import jax.numpy as jnp
from jax import lax
from jax.experimental import pallas as pl
from jax.experimental.pallas import tpu as pltpu

F32 = jnp.float32
BF16 = jnp.bfloat16

D_MODEL = 2048
DEPTH = 2
POOL_DIM = 512
POOL_WINDOWS = (2, 4, 8, 16)
POOL_GROUP_DIM = 128
POOL_HIST = 15
POOL_PAD = 16
SGU_DIM = 1024
SGU_GROUPS = 8
SGU_GROUP_DIM = 128
SGU_CHUNK = 128
CONV_DIM = 512
CONV_WIDTH = 3
CONV_PAD = 8
D_FF = 5632
PLE_DIM = 256
ALPHA = (2 * DEPTH) ** 0.25
LN_EPS = 1e-5
IN_COLS = 10240
GATE_COLS = 3 * D_MODEL

TN_IN = 1024
NC_IN = 256
MC_IN = 512
MC_LN = 256
TM_MIX = 256
TM_FFN = 1024
MC_FFN = 512
TF_FFN = 512
N_GATE = D_MODEL // TF_FFN
CAST_ROWS = 512
VMEM_LIMIT = 56 * 1024 * 1024


def _layer_norm(x, g, b):
    mu = jnp.mean(x, axis=-1, keepdims=True)
    xc = x - mu
    var = jnp.mean(xc * xc, axis=-1, keepdims=True)
    return xc * lax.rsqrt(var + LN_EPS) * g + b


def _gelu(x):
    return 0.5 * x * (1.0 + lax.erf(x * (2.0 ** -0.5)))


def _sigmoid(x):
    return 0.5 * jnp.tanh(0.5 * x) + 0.5


def _const_spec(shape):
    nd = len(shape)
    return pl.BlockSpec(shape, lambda *_: (0,) * nd, pipeline_mode=pl.Buffered(1))


def _inproj_body(x_ref, w_ref, lng_ref, lnb_ref,
                 a_ref, cb_ref, conv_ref, v_ref, u_ref, g_ref, *scratch):
    xb_ref = scratch[0] if scratch else x_ref
    j = pl.program_id(1)
    tm = x_ref.shape[0]
    mc = min(MC_IN, tm)
    nchunks = TN_IN // NC_IN
    half = nchunks // 2
    row_chunks = [pl.ds(m, mc) for m in range(0, tm, mc)]
    ln_chunks = [pl.ds(m, min(MC_LN, tm)) for m in range(0, tm, min(MC_LN, tm))]

    def cols(n):
        return slice(n * NC_IN, (n + 1) * NC_IN)

    def z(rows, n):
        return jnp.dot(xb_ref[rows, :], w_ref[:, cols(n)], preferred_element_type=F32)

    @pl.when(j == 0)
    def _():
        for rows in row_chunks:
            if scratch:
                xb_ref[rows, :] = x_ref[rows, :].astype(BF16)
            for n in range(nchunks):
                u_ref[rows, cols(n)] = _gelu(z(rows, n)).astype(u_ref.dtype)

    @pl.when(j == 1)
    def _():
        for rows in ln_chunks:
            v = jnp.concatenate([_gelu(z(rows, n)) for n in range(nchunks)], axis=-1)
            v_ref[rows, :] = _layer_norm(v, lng_ref[...], lnb_ref[...]).astype(v_ref.dtype)

    @pl.when(j == 2)
    def _():
        for rows in row_chunks:
            for n in range(half):
                a_ref[rows, cols(n)] = z(rows, n)
            for n in range(half):
                cb_ref[rows, cols(n)] = z(rows, half + n).astype(cb_ref.dtype)

    @pl.when(j == 3)
    def _():
        for rows in row_chunks:
            for n in range(half):
                conv_ref[rows, cols(n)] = z(rows, n) * z(rows, half + n)

    @pl.when(j >= 4)
    def _():
        for rows in row_chunks:
            for n in range(nchunks):
                g_ref[rows, cols(n)] = _sigmoid(z(rows, n)).astype(g_ref.dtype)


def _inproj(x, w_in_p, ln_g, ln_b, *, tm, v_dtype):
    n = x.shape[0]
    assert n % tm == 0
    n_tiles = n // tm
    nj = IN_COLS // TN_IN

    def early(k):
        return lambda i, j: (jnp.minimum(i + (j > k).astype(jnp.int32), n_tiles - 1), 0)

    out_shape = (
        jax.ShapeDtypeStruct((n, POOL_DIM), F32),
        jax.ShapeDtypeStruct((n, CONV_DIM), BF16),
        jax.ShapeDtypeStruct((n, CONV_DIM), F32),
        jax.ShapeDtypeStruct((n, SGU_DIM), v_dtype),
        jax.ShapeDtypeStruct((n, SGU_DIM), BF16),
        jax.ShapeDtypeStruct((n, GATE_COLS), BF16),
    )
    return pl.pallas_call(
        _inproj_body,
        grid=(n_tiles, nj),
        in_specs=[
            pl.BlockSpec((tm, D_MODEL), lambda i, j: (i, 0)),
            pl.BlockSpec((D_MODEL, TN_IN), lambda i, j: (0, j)),
            _const_spec((1, SGU_DIM)),
            _const_spec((1, SGU_DIM)),
        ],
        out_specs=(
            pl.BlockSpec((tm, POOL_DIM), early(2)),
            pl.BlockSpec((tm, CONV_DIM), early(2)),
            pl.BlockSpec((tm, CONV_DIM), early(3)),
            pl.BlockSpec((tm, SGU_DIM), early(1)),
            pl.BlockSpec((tm, SGU_DIM), early(0)),
            pl.BlockSpec((tm, TN_IN), lambda i, j: (i, jnp.maximum(j - 4, 0))),
        ),
        out_shape=out_shape,
        scratch_shapes=[] if x.dtype == BF16 else [pltpu.VMEM((tm, D_MODEL), BF16)],
        compiler_params=pltpu.CompilerParams(
            dimension_semantics=("arbitrary", "arbitrary"), vmem_limit_bytes=VMEM_LIMIT),
        name="inproj",
    )(x, w_in_p, ln_g, ln_b)


def _mixer_body(nseq, seq_len, pos0, tiles_per_seq, has_hist, *refs):
    if has_hist:
        (x_ref, a_ref, cb_ref, cv_ref, v_ref, u_ref, ga_ref, gb_ref, gc_ref, hp_ref, hc_ref,
         pw_ref, ps_ref, bd_ref, sb_ref, cw_ref, wa_ref, wb_ref, wc_ref, wo_ref, lg_ref, lb_ref,
         x1_ref, x1b_ref, np_ref, nc_ref, p_scr, q_scr) = refs
    else:
        (x_ref, a_ref, cb_ref, cv_ref, v_ref, u_ref, ga_ref, gb_ref, gc_ref,
         pw_ref, ps_ref, bd_ref, sb_ref, cw_ref, wa_ref, wb_ref, wc_ref, wo_ref, lg_ref, lb_ref,
         x1_ref, x1b_ref, np_ref, nc_ref, p_scr, q_scr) = refs
    L = seq_len
    R = nseq * L
    t = pl.program_id(0) % tiles_per_seq

    if has_hist:
        p_scr[:, 0:POOL_PAD, :] = hp_ref[...]
        q_scr[:, 0:CONV_PAD, :] = hc_ref[...]
    else:
        @pl.when(t == 0)
        def _():
            p_scr[:, 0:POOL_PAD, :] = jnp.zeros((nseq, POOL_PAD, POOL_DIM), F32)
            q_scr[:, 0:CONV_PAD, :] = jnp.zeros((nseq, CONV_PAD, CONV_DIM), F32)

        @pl.when(t != 0)
        def _():
            p_scr[:, 0:POOL_PAD, :] = p_scr[:, L:L + POOL_PAD, :]
            q_scr[:, 0:CONV_PAD, :] = q_scr[:, L:L + CONV_PAD, :]
    p_scr[:, POOL_PAD:, :] = a_ref[...].reshape(nseq, L, POOL_DIM)
    q_scr[:, CONV_PAD:, :] = cv_ref[...].reshape(nseq, L, CONV_DIM)

    pos = pos0 + t * L + lax.broadcasted_iota(jnp.int32, (1, L, 1), 1) + 1
    ya = []
    for g, win in enumerate(POOL_WINDOWS):
        cols = slice(g * POOL_GROUP_DIM, (g + 1) * POOL_GROUP_DIM)
        cur = p_scr[:, POOL_PAD:POOL_PAD + L, cols]
        s = cur
        for k in range(1, win):
            s = s + p_scr[:, POOL_PAD - k:POOL_PAD - k + L, cols]
        inv_cnt = 1.0 / jnp.minimum(pos, win).astype(F32)
        pooled = (s * inv_cnt - cur).reshape(R, POOL_GROUP_DIM).astype(BF16)
        ya.append(jnp.dot(pooled, pw_ref[g], preferred_element_type=F32) * ps_ref[:, cols])
    ya = jnp.concatenate(ya, axis=-1).astype(BF16)

    vb = v_ref[...].astype(BF16)
    yb = []
    for g in range(SGU_GROUPS):
        cols = slice(g * SGU_GROUP_DIM, (g + 1) * SGU_GROUP_DIM)
        s = jnp.dot(bd_ref[g], vb[:, cols], preferred_element_type=F32) + sb_ref[:, cols]
        yb.append(u_ref[:, cols].astype(F32) * s)
    yb = jnp.concatenate(yb, axis=-1).astype(BF16)

    conv = q_scr[:, CONV_PAD - 2:CONV_PAD - 2 + L, :] * cw_ref[0:1, :]
    for k in range(1, CONV_WIDTH):
        o = CONV_PAD - 2 + k
        conv = conv + q_scr[:, o:o + L, :] * cw_ref[k:k + 1, :]
    yc = (cb_ref[...].astype(F32) * conv.reshape(R, CONV_DIM)).astype(BF16)

    merged = ga_ref[...].astype(F32) * jnp.dot(ya, wa_ref[...], preferred_element_type=F32)
    merged = merged + gb_ref[...].astype(F32) * jnp.dot(yb, wb_ref[...], preferred_element_type=F32)
    merged = merged + gc_ref[...].astype(F32) * jnp.dot(yc, wc_ref[...], preferred_element_type=F32)
    o = jnp.dot(merged.astype(BF16), wo_ref[...], preferred_element_type=F32)
    x1 = _layer_norm(ALPHA * x_ref[...] + o, lg_ref[...], lb_ref[...])
    x1_ref[...] = x1
    x1b_ref[...] = x1.astype(BF16)

    if has_hist:
        np_ref[...] = p_scr[:, L:L + POOL_PAD, :]
        nc_ref[...] = q_scr[:, L:L + CONV_PAD, :]
    else:
        @pl.when(t == tiles_per_seq - 1)
        def _():
            np_ref[...] = p_scr[:, L:L + POOL_PAD, :]
            nc_ref[...] = q_scr[:, L:L + CONV_PAD, :]


def _mixer(x, acts, lw, sgu_bd, sgu_bias, *, nseq, seq_len, pos0, tiles_per_seq, hist=None):
    a, cb, cv, v, u, gates = acts
    n = x.shape[0]
    R = nseq * seq_len
    assert R == TM_MIX and n % R == 0
    n_tiles = n // R
    row = lambda i: (i, 0)
    has_hist = hist is not None
    n_state_seq = n_tiles * nseq // tiles_per_seq
    state_idx = lambda i: (i // tiles_per_seq, 0, 0)

    in_specs = [
        pl.BlockSpec((R, D_MODEL), row),
        pl.BlockSpec((R, POOL_DIM), row),
        pl.BlockSpec((R, CONV_DIM), row),
        pl.BlockSpec((R, CONV_DIM), row),
        pl.BlockSpec((R, SGU_DIM), row),
        pl.BlockSpec((R, SGU_DIM), row),
        pl.BlockSpec((R, D_MODEL), lambda i: (i, 0)),
        pl.BlockSpec((R, D_MODEL), lambda i: (i, 1)),
        pl.BlockSpec((R, D_MODEL), lambda i: (i, 2)),
    ]
    args = [x, a, cb, cv, v, u, gates, gates, gates]
    if has_hist:
        hp, hc = hist
        in_specs += [pl.BlockSpec((nseq, POOL_PAD, POOL_DIM), state_idx),
                     pl.BlockSpec((nseq, CONV_PAD, CONV_DIM), state_idx)]
        args += [hp, hc]
    weights = [lw["pool_w"], lw["pool_scale"], sgu_bd, sgu_bias, lw["conv_w"],
               lw["w_br_a"], lw["w_br_b"], lw["w_br_c"], lw["w_o"], lw["ln1_g"], lw["ln1_b"]]
    in_specs += [_const_spec(w.shape) for w in weights]
    args += weights

    out_shape = (
        jax.ShapeDtypeStruct((n, D_MODEL), F32),
        jax.ShapeDtypeStruct((n, D_MODEL), BF16),
        jax.ShapeDtypeStruct((n_state_seq, POOL_PAD, POOL_DIM), F32),
        jax.ShapeDtypeStruct((n_state_seq, CONV_PAD, CONV_DIM), F32),
    )
    out_specs = (
        pl.BlockSpec((R, D_MODEL), row),
        pl.BlockSpec((R, D_MODEL), row),
        pl.BlockSpec((nseq, POOL_PAD, POOL_DIM), state_idx),
        pl.BlockSpec((nseq, CONV_PAD, CONV_DIM), state_idx),
    )
    return pl.pallas_call(
        functools.partial(_mixer_body, nseq, seq_len, pos0, tiles_per_seq, has_hist),
        grid=(n_tiles,),
        in_specs=in_specs,
        out_specs=out_specs,
        out_shape=out_shape,
        scratch_shapes=[pltpu.VMEM((nseq, POOL_PAD + seq_len, POOL_DIM), F32),
                        pltpu.VMEM((nseq, CONV_PAD + seq_len, CONV_DIM), F32)],
        compiler_params=pltpu.CompilerParams(
            dimension_semantics=("arbitrary",), vmem_limit_bytes=VMEM_LIMIT),
        name="mixer_sample" if has_hist else "mixer_prompt",
    )(*args)


def _ffn_body(x_ref, xb_ref, p_ref, wg_ref, wu_ref, wd_ref, wpg_ref, wpe_ref, lg_ref, lb_ref, o_ref):
    j = pl.program_id(1)
    last = pl.num_programs(1) - 1
    tm = xb_ref.shape[0]
    mc = min(MC_FFN, tm)
    row_chunks = [pl.ds(m, mc) for m in range(0, tm, mc)]
    res_rows = x_ref.shape[0]

    def swiglu_down(rows):
        xb = xb_ref[rows, :]
        hg = jnp.dot(xb, wg_ref[...], preferred_element_type=F32)
        hu = jnp.dot(xb, wu_ref[...], preferred_element_type=F32)
        act = (hg * _sigmoid(hg) * hu).astype(BF16)
        return jnp.dot(act, wd_ref[...], preferred_element_type=F32)

    for c in range(N_GATE):
        @pl.when(j == c)
        def _(c=c):
            cols = slice(c * TF_FFN, (c + 1) * TF_FFN)
            for rows in row_chunks:
                if c == 0:
                    o_ref[rows, :] = swiglu_down(rows)
                else:
                    o_ref[rows, :] += swiglu_down(rows)
                gate = _sigmoid(jnp.dot(xb_ref[rows, :], wpg_ref[...], preferred_element_type=F32))
                emb = jnp.dot(p_ref[rows, :].astype(BF16), wpe_ref[:, cols],
                              preferred_element_type=F32)
                o_ref[rows, cols] += gate * emb
            o_ref[pl.ds(c * res_rows, res_rows), :] += ALPHA * x_ref[...]

    @pl.when((j >= N_GATE) & (j != last))
    def _():
        for rows in row_chunks:
            o_ref[rows, :] += swiglu_down(rows)

    @pl.when(j == last)
    def _():
        mc_ln = min(MC_LN, tm)
        for m in range(0, tm, mc_ln):
            rows = pl.ds(m, mc_ln)
            o_ref[rows, :] = _layer_norm(o_ref[rows, :] + swiglu_down(rows), lg_ref[...], lb_ref[...])


def _ffn(x1, x1b, p_all, p_row0, lw, *, tm):
    n = x1.shape[0]
    assert n % tm == 0 and p_row0 % tm == 0 and tm % min(MC_FFN, tm) == 0 and tm % N_GATE == 0
    nj = D_FF // TF_FFN
    assert N_GATE < nj - 1
    gate_step = lambda j: jnp.minimum(j, N_GATE - 1)
    p_off = p_row0 // tm
    row = lambda i, j: (i, 0)
    return pl.pallas_call(
        _ffn_body,
        grid=(n // tm, nj),
        in_specs=[
            pl.BlockSpec((tm // N_GATE, D_MODEL), lambda i, j: (i * N_GATE + gate_step(j), 0)),
            pl.BlockSpec((tm, D_MODEL), row),
            pl.BlockSpec((tm, PLE_DIM), lambda i, j: (i + p_off, 0)),
            pl.BlockSpec((D_MODEL, TF_FFN), lambda i, j: (0, j)),
            pl.BlockSpec((D_MODEL, TF_FFN), lambda i, j: (0, j + nj)),
            pl.BlockSpec((TF_FFN, D_MODEL), lambda i, j: (j, 0)),
            pl.BlockSpec((D_MODEL, TF_FFN), lambda i, j: (0, gate_step(j))),
            _const_spec((PLE_DIM, D_MODEL)),
            _const_spec((1, D_MODEL)),
            _const_spec((1, D_MODEL)),
        ],
        out_specs=pl.BlockSpec((tm, D_MODEL), row),
        out_shape=jax.ShapeDtypeStruct((n, D_MODEL), F32),
        compiler_params=pltpu.CompilerParams(
            dimension_semantics=("arbitrary", "arbitrary"), vmem_limit_bytes=VMEM_LIMIT),
        name="ffn",
    )(x1, x1b, p_all, lw["w_gu"], lw["w_gu"], lw["w_down"], lw["w_pe_gate"], lw["w_pe"],
      lw["ln2_g"], lw["ln2_b"])


def _cast_body(w_ref, o_ref):
    o_ref[...] = w_ref[...].astype(o_ref.dtype)


def _cast_bf16(w, layer, block, src_col_block=None):
    _, rows, cols = w.shape
    br, bc = block
    assert rows % br == 0 and cols % bc == 0
    src = src_col_block if src_col_block is not None else (lambda j: j)
    return pl.pallas_call(
        _cast_body,
        grid=(rows // br, cols // bc),
        in_specs=[pl.BlockSpec((None, br, bc), lambda i, j: (layer, i, src(j)))],
        out_specs=pl.BlockSpec((br, bc), lambda i, j: (i, j)),
        out_shape=jax.ShapeDtypeStruct((rows, cols), BF16),
        compiler_params=pltpu.CompilerParams(
            dimension_semantics=("arbitrary", "arbitrary"), vmem_limit_bytes=VMEM_LIMIT),
        name="cast_bf16",
    )(w)


def _w_in_src_block(j):
    n_uv = 2 * SGU_DIM // POOL_DIM
    return jnp.where(j < n_uv, j + 1, jnp.where(j == n_uv, 0, j))


def _sgu_block_diag(sgu_w, sgu_b, chunk, rows):
    w = jnp.tril(sgu_w[:, :chunk, :chunk])
    reps = rows // chunk
    eye = jnp.eye(reps, dtype=w.dtype)
    bd = jnp.einsum("ab,gts->gatbs", eye, w).reshape(SGU_GROUPS, rows, rows)
    bias = jnp.tile(jnp.transpose(sgu_b[:, :chunk]), (reps, 1))
    bias = jnp.repeat(bias, SGU_GROUP_DIM, axis=1)
    return bd.astype(BF16), bias.astype(F32)


def _layer_weights(i, w_in, pool_w, pool_scale, sgu_ln_g, sgu_ln_b, sgu_w, sgu_b, conv_w, w_br_a,
                   w_br_b, w_br_c, w_o, ln1_g, ln1_b, w_gu, w_down, w_pe, w_pe_gate, ln2_g, ln2_b,
                   sample_len):
    wide = (CAST_ROWS, D_MODEL)
    row = lambda a: a[i].reshape(1, -1).astype(F32)
    return dict(
        w_in=_cast_bf16(w_in, i, (D_MODEL, POOL_DIM), _w_in_src_block),
        sgu_ln_g=row(sgu_ln_g), sgu_ln_b=row(sgu_ln_b),
        pool_w=pool_w[i].astype(BF16), pool_scale=row(pool_scale),
        sgu_prompt=_sgu_block_diag(sgu_w[i], sgu_b[i], SGU_CHUNK, TM_MIX),
        sgu_sample=_sgu_block_diag(sgu_w[i], sgu_b[i], sample_len, TM_MIX),
        conv_w=conv_w[i].astype(F32),
        w_br_a=_cast_bf16(w_br_a, i, wide), w_br_b=_cast_bf16(w_br_b, i, wide),
        w_br_c=_cast_bf16(w_br_c, i, wide), w_o=_cast_bf16(w_o, i, wide),
        ln1_g=row(ln1_g), ln1_b=row(ln1_b),
        w_gu=_cast_bf16(w_gu, i, (D_MODEL, TF_FFN)), w_down=_cast_bf16(w_down, i, wide),
        w_pe=_cast_bf16(w_pe, i, (PLE_DIM, D_MODEL)), w_pe_gate=_cast_bf16(w_pe_gate, i, wide),
        ln2_g=row(ln2_g), ln2_b=row(ln2_b),
    )


def kernel(x_prompt, x_sample, state_pool, state_conv, p_prompt, p_sample, w_in, pool_w, pool_scale,
           sgu_ln_g, sgu_ln_b, sgu_w, sgu_b, conv_w, w_br_a, w_br_b, w_br_c, w_o, ln1_g, ln1_b, w_gu,
           w_down, w_pe, w_pe_gate, ln2_g, ln2_b):
    batch, seq, _ = x_prompt.shape
    dec_batch, dec_seq, _ = x_sample.shape
    n_p = batch * seq
    n_s = dec_batch * dec_seq
    tm_in_p = 1024
    assert seq % TM_MIX == 0 and TM_MIX % dec_seq == 0 and n_s % TM_MIX == 0
    assert n_p % tm_in_p == 0 and n_p % TM_FFN == 0

    xp = x_prompt.reshape(n_p, D_MODEL)
    xs = x_sample.reshape(n_s, D_MODEL)
    pp_all = p_prompt.reshape(DEPTH * n_p, PLE_DIM)
    ps_all = p_sample.reshape(DEPTH * n_s, PLE_DIM)
    hist_pool = jnp.pad(state_pool, ((0, 0), (0, 0), (POOL_PAD - POOL_HIST, 0), (0, 0)))
    hist_conv = jnp.pad(state_conv, ((0, 0), (0, 0), (CONV_PAD - (CONV_WIDTH - 1), 0), (0, 0)))

    outs = {k: [] for k in ("pool_p", "conv_p", "pool_s", "conv_s", "v_s")}
    for i in range(DEPTH):
        lw = _layer_weights(i, w_in, pool_w, pool_scale, sgu_ln_g, sgu_ln_b, sgu_w, sgu_b, conv_w,
                            w_br_a, w_br_b, w_br_c, w_o, ln1_g, ln1_b, w_gu, w_down, w_pe,
                            w_pe_gate, ln2_g, ln2_b, dec_seq)
        acts = _inproj(xp, lw["w_in"], lw["sgu_ln_g"], lw["sgu_ln_b"], tm=tm_in_p, v_dtype=BF16)
        x1, x1b, pool_p, conv_p = _mixer(
            xp, acts, lw, *lw["sgu_prompt"], nseq=1, seq_len=TM_MIX, pos0=0,
            tiles_per_seq=seq // TM_MIX)
        xp = _ffn(x1, x1b, pp_all, i * n_p, lw, tm=TM_FFN)
        acts = _inproj(xs, lw["w_in"], lw["sgu_ln_g"], lw["sgu_ln_b"], tm=n_s, v_dtype=F32)
        x1, x1b, pool_s, conv_s = _mixer(
            xs, acts, lw, *lw["sgu_sample"], nseq=TM_MIX // dec_seq, seq_len=dec_seq, pos0=1024,
            tiles_per_seq=1, hist=(hist_pool[i], hist_conv[i]))
        xs = _ffn(x1, x1b, ps_all, i * n_s, lw, tm=n_s)

        outs["pool_p"].append(pool_p[:, POOL_PAD - POOL_HIST:])
        outs["conv_p"].append(conv_p[:, CONV_PAD - (CONV_WIDTH - 1):])
        outs["pool_s"].append(pool_s[:, POOL_PAD - POOL_HIST:])
        outs["conv_s"].append(conv_s[:, CONV_PAD - (CONV_WIDTH - 1):])
        outs["v_s"].append(acts[3].reshape(dec_batch, dec_seq, SGU_DIM))

    return (xp.reshape(batch, seq, D_MODEL), xs.reshape(dec_batch, dec_seq, D_MODEL),
            jnp.stack(outs["pool_p"]), jnp.stack(outs["conv_p"]), jnp.stack(outs["pool_s"]),
            jnp.stack(outs["conv_s"]), jnp.stack(outs["v_s"]))
```

```python
import functools

import jax
import jax.numpy as jnp
from jax import lax
from jax.experimental import pallas as pl
from jax.experimental.pallas import tpu as pltpu

F32 = jnp.float32
BF16 = jnp.bfloat16

D_MODEL = 2048
DEPTH = 2
POOL_DIM = 512
POOL_WINDOWS = (2, 4, 8, 16)
POOL_GROUP_DIM = 128
POOL_HIST = 15
POOL_PAD = 16
SGU_DIM = 1024
SGU_GROUPS = 8
SGU_GROUP_DIM = 128
SGU_CHUNK = 128
CONV_DIM = 512
CONV_WIDTH = 3
CONV_PAD = 8
D_FF = 5632
PLE_DIM = 256
ALPHA = (2 * DEPTH) ** 0.25
LN_EPS = 1e-5
IN_COLS = 10240
GATE_COLS = 3 * D_MODEL

TN_IN = 1024
NC_IN = 256
MC_IN = 512
MC_LN = 256
TM_MIX = 256
TM_FFN = 1024
MC_FFN = 512
TF_FFN = 512
N_GATE = D_MODEL // TF_FFN
CAST_ROWS = 512
VMEM_LIMIT = 56 * 1024 * 1024


def _layer_norm(x, g, b):
    mu = jnp.mean(x, axis=-1, keepdims=True)
    xc = x - mu
    var = jnp.mean(xc * xc, axis=-1, keepdims=True)
    return xc * lax.rsqrt(var + LN_EPS) * g + b


def _gelu(x):
    return 0.5 * x * (1.0 + lax.erf(x * (2.0 ** -0.5)))


def _sigmoid(x):
    return 0.5 * jnp.tanh(0.5 * x) + 0.5


def _const_spec(shape):
    nd = len(shape)
    return pl.BlockSpec(shape, lambda *_: (0,) * nd, pipeline_mode=pl.Buffered(1))


def _inproj_body(x_ref, w_ref, lng_ref, lnb_ref,
                 a_ref, cb_ref, conv_ref, v_ref, u_ref, g_ref, *scratch):
    xb_ref = scratch[0] if scratch else x_ref
    j = pl.program_id(1)
    tm = x_ref.shape[0]
    mc = min(MC_IN, tm)
    nchunks = TN_IN // NC_IN
    half = nchunks // 2
    row_chunks = [pl.ds(m, mc) for m in range(0, tm, mc)]
    ln_chunks = [pl.ds(m, min(MC_LN, tm)) for m in range(0, tm, min(MC_LN, tm))]

    def cols(n):
        return slice(n * NC_IN, (n + 1) * NC_IN)

    def z(rows, n):
        return jnp.dot(xb_ref[rows, :], w_ref[:, cols(n)], preferred_element_type=F32)

    @pl.when(j == 0)
    def _():
        for rows in row_chunks:
            if scratch:
                xb_ref[rows, :] = x_ref[rows, :].astype(BF16)
            for n in range(nchunks):
                u_ref[rows, cols(n)] = _gelu(z(rows, n)).astype(u_ref.dtype)

    @pl.when(j == 1)
    def _():
        for rows in ln_chunks:
            v = jnp.concatenate([_gelu(z(rows, n)) for n in range(nchunks)], axis=-1)
            v_ref[rows, :] = _layer_norm(v, lng_ref[...], lnb_ref[...]).astype(v_ref.dtype)

    @pl.when(j == 2)
    def _():
        for rows in row_chunks:
            for n in range(half):
                a_ref[rows, cols(n)] = z(rows, n)
            for n in range(half):
                cb_ref[rows, cols(n)] = z(rows, half + n).astype(cb_ref.dtype)

    @pl.when(j == 3)
    def _():
        for rows in row_chunks:
            for n in range(half):
                conv_ref[rows, cols(n)] = z(rows, n) * z(rows, half + n)

    @pl.when(j >= 4)
    def _():
        for rows in row_chunks:
            for n in range(nchunks):
                g_ref[rows, cols(n)] = _sigmoid(z(rows, n)).astype(g_ref.dtype)


def _inproj(x, w_in_p, ln_g, ln_b, *, tm, v_dtype):
    n = x.shape[0]
    assert n % tm == 0
    n_tiles = n // tm
    nj = IN_COLS // TN_IN

    def early(k):
        return lambda i, j: (jnp.minimum(i + (j > k).astype(jnp.int32), n_tiles - 1), 0)

    out_shape = (
        jax.ShapeDtypeStruct((n, POOL_DIM), F32),
        jax.ShapeDtypeStruct((n, CONV_DIM), BF16),
        jax.ShapeDtypeStruct((n, CONV_DIM), F32),
        jax.ShapeDtypeStruct((n, SGU_DIM), v_dtype),
        jax.ShapeDtypeStruct((n, SGU_DIM), BF16),
        jax.ShapeDtypeStruct((n, GATE_COLS), BF16),
    )
    return pl.pallas_call(
        _inproj_body,
        grid=(n_tiles, nj),
        in_specs=[
            pl.BlockSpec((tm, D_MODEL), lambda i, j: (i, 0)),
            pl.BlockSpec((D_MODEL, TN_IN), lambda i, j: (0, j)),
            _const_spec((1, SGU_DIM)),
            _const_spec((1, SGU_DIM)),
        ],
        out_specs=(
            pl.BlockSpec((tm, POOL_DIM), early(2)),
            pl.BlockSpec((tm, CONV_DIM), early(2)),
            pl.BlockSpec((tm, CONV_DIM), early(3)),
            pl.BlockSpec((tm, SGU_DIM), early(1)),
            pl.BlockSpec((tm, SGU_DIM), early(0)),
            pl.BlockSpec((tm, TN_IN), lambda i, j: (i, jnp.maximum(j - 4, 0))),
        ),
        out_shape=out_shape,
        scratch_shapes=[] if x.dtype == BF16 else [pltpu.VMEM((tm, D_MODEL), BF16)],
        compiler_params=pltpu.CompilerParams(
            dimension_semantics=("arbitrary", "arbitrary"), vmem_limit_bytes=VMEM_LIMIT),
        name="inproj",
    )(x, w_in_p, ln_g, ln_b)


def _mixer_body(nseq, seq_len, pos0, tiles_per_seq, has_hist, *refs):
    if has_hist:
        (x_ref, a_ref, cb_ref, cv_ref, v_ref, u_ref, ga_ref, gb_ref, gc_ref, hp_ref, hc_ref,
         pw_ref, ps_ref, bd_ref, sb_ref, cw_ref, wa_ref, wb_ref, wc_ref, wo_ref, lg_ref, lb_ref,
         x1_ref, x1b_ref, np_ref, nc_ref, p_scr, q_scr) = refs
    else:
        (x_ref, a_ref, cb_ref, cv_ref, v_ref, u_ref, ga_ref, gb_ref, gc_ref,
         pw_ref, ps_ref, bd_ref, sb_ref, cw_ref, wa_ref, wb_ref, wc_ref, wo_ref, lg_ref, lb_ref,
         x1_ref, x1b_ref, np_ref, nc_ref, p_scr, q_scr) = refs
    L = seq_len
    R = nseq * L
    t = pl.program_id(0) % tiles_per_seq

    if has_hist:
        p_scr[:, 0:POOL_PAD, :] = hp_ref[...]
        q_scr[:, 0:CONV_PAD, :] = hc_ref[...]
    else:
        @pl.when(t == 0)
        def _():
            p_scr[:, 0:POOL_PAD, :] = jnp.zeros((nseq, POOL_PAD, POOL_DIM), F32)
            q_scr[:, 0:CONV_PAD, :] = jnp.zeros((nseq, CONV_PAD, CONV_DIM), F32)

        @pl.when(t != 0)
        def _():
            p_scr[:, 0:POOL_PAD, :] = p_scr[:, L:L + POOL_PAD, :]
            q_scr[:, 0:CONV_PAD, :] = q_scr[:, L:L + CONV_PAD, :]
    p_scr[:, POOL_PAD:, :] = a_ref[...].reshape(nseq, L, POOL_DIM)
    q_scr[:, CONV_PAD:, :] = cv_ref[...].reshape(nseq, L, CONV_DIM)

    pos = pos0 + t * L + lax.broadcasted_iota(jnp.int32, (1, L, 1), 1) + 1
    ya = []
    for g, win in enumerate(POOL_WINDOWS):
        cols = slice(g * POOL_GROUP_DIM, (g + 1) * POOL_GROUP_DIM)
        cur = p_scr[:, POOL_PAD:POOL_PAD + L, cols]
        s = cur
        for k in range(1, win):
            s = s + p_scr[:, POOL_PAD - k:POOL_PAD - k + L, cols]
        inv_cnt = 1.0 / jnp.minimum(pos, win).astype(F32)
        pooled = (s * inv_cnt - cur).reshape(R, POOL_GROUP_DIM).astype(BF16)
        ya.append(jnp.dot(pooled, pw_ref[g], preferred_element_type=F32) * ps_ref[:, cols])
    ya = jnp.concatenate(ya, axis=-1).astype(BF16)

    vb = v_ref[...].astype(BF16)
    yb = []
    for g in range(SGU_GROUPS):
        cols = slice(g * SGU_GROUP_DIM, (g + 1) * SGU_GROUP_DIM)
        s = jnp.dot(bd_ref[g], vb[:, cols], preferred_element_type=F32) + sb_ref[:, cols]
        yb.append(u_ref[:, cols].astype(F32) * s)
    yb = jnp.concatenate(yb, axis=-1).astype(BF16)

    conv = q_scr[:, CONV_PAD - 2:CONV_PAD - 2 + L, :] * cw_ref[0:1, :]
    for k in range(1, CONV_WIDTH):
        o = CONV_PAD - 2 + k
        conv = conv + q_scr[:, o:o + L, :] * cw_ref[k:k + 1, :]
    yc = (cb_ref[...].astype(F32) * conv.reshape(R, CONV_DIM)).astype(BF16)

    merged = ga_ref[...].astype(F32) * jnp.dot(ya, wa_ref[...], preferred_element_type=F32)
    merged = merged + gb_ref[...].astype(F32) * jnp.dot(yb, wb_ref[...], preferred_element_type=F32)
    merged = merged + gc_ref[...].astype(F32) * jnp.dot(yc, wc_ref[...], preferred_element_type=F32)
    o = jnp.dot(merged.astype(BF16), wo_ref[...], preferred_element_type=F32)
    x1 = _layer_norm(ALPHA * x_ref[...] + o, lg_ref[...], lb_ref[...])
    x1_ref[...] = x1
    x1b_ref[...] = x1.astype(BF16)

    if has_hist:
        np_ref[...] = p_scr[:, L:L + POOL_PAD, :]
        nc_ref[...] = q_scr[:, L:L + CONV_PAD, :]
    else:
        @pl.when(t == tiles_per_seq - 1)
        def _():
            np_ref[...] = p_scr[:, L:L + POOL_PAD, :]
            nc_ref[...] = q_scr[:, L:L + CONV_PAD, :]


def _mixer(x, acts, lw, sgu_bd, sgu_bias, *, nseq, seq_len, pos0, tiles_per_seq, hist=None):
    a, cb, cv, v, u, gates = acts
    n = x.shape[0]
    R = nseq * seq_len
    assert R == TM_MIX and n % R == 0
    n_tiles = n // R
    row = lambda i: (i, 0)
    has_hist = hist is not None
    n_state_seq = n_tiles * nseq // tiles_per_seq
    state_idx = lambda i: (i // tiles_per_seq, 0, 0)

    in_specs = [
        pl.BlockSpec((R, D_MODEL), row),
        pl.BlockSpec((R, POOL_DIM), row),
        pl.BlockSpec((R, CONV_DIM), row),
        pl.BlockSpec((R, CONV_DIM), row),
        pl.BlockSpec((R, SGU_DIM), row),
        pl.BlockSpec((R, SGU_DIM), row),
        pl.BlockSpec((R, D_MODEL), lambda i: (i, 0)),
        pl.BlockSpec((R, D_MODEL), lambda i: (i, 1)),
        pl.BlockSpec((R, D_MODEL), lambda i: (i, 2)),
    ]
    args = [x, a, cb, cv, v, u, gates, gates, gates]
    if has_hist:
        hp, hc = hist
        in_specs += [pl.BlockSpec((nseq, POOL_PAD, POOL_DIM), state_idx),
                     pl.BlockSpec((nseq, CONV_PAD, CONV_DIM), state_idx)]
        args += [hp, hc]
    weights = [lw["pool_w"], lw["pool_scale"], sgu_bd, sgu_bias, lw["conv_w"],
               lw["w_br_a"], lw["w_br_b"], lw["w_br_c"], lw["w_o"], lw["ln1_g"], lw["ln1_b"]]
    in_specs += [_const_spec(w.shape) for w in weights]
    args += weights

    out_shape = (
        jax.ShapeDtypeStruct((n, D_MODEL), F32),
        jax.ShapeDtypeStruct((n, D_MODEL), BF16),
        jax.ShapeDtypeStruct((n_state_seq, POOL_PAD, POOL_DIM), F32),
        jax.ShapeDtypeStruct((n_state_seq, CONV_PAD, CONV_DIM), F32),
    )
    out_specs = (
        pl.BlockSpec((R, D_MODEL), row),
        pl.BlockSpec((R, D_MODEL), row),
        pl.BlockSpec((nseq, POOL_PAD, POOL_DIM), state_idx),
        pl.BlockSpec((nseq, CONV_PAD, CONV_DIM), state_idx),
    )
    return pl.pallas_call(
        functools.partial(_mixer_body, nseq, seq_len, pos0, tiles_per_seq, has_hist),
        grid=(n_tiles,),
        in_specs=in_specs,
        out_specs=out_specs,
        out_shape=out_shape,
        scratch_shapes=[pltpu.VMEM((nseq, POOL_PAD + seq_len, POOL_DIM), F32),
                        pltpu.VMEM((nseq, CONV_PAD + seq_len, CONV_DIM), F32)],
        compiler_params=pltpu.CompilerParams(
            dimension_semantics=("arbitrary",), vmem_limit_bytes=VMEM_LIMIT),
        name="mixer_sample" if has_hist else "mixer_prompt",
    )(*args)


def _ffn_body(x_ref, xb_ref, p_ref, wg_ref, wu_ref, wd_ref, wpg_ref, wpe_ref, lg_ref, lb_ref, o_ref):
    j = pl.program_id(1)
    last = pl.num_programs(1) - 1
    tm = xb_ref.shape[0]
    mc = min(MC_FFN, tm)
    row_chunks = [pl.ds(m, mc) for m in range(0, tm, mc)]
    res_rows = x_ref.shape[0]

    def swiglu_down(rows):
        xb = xb_ref[rows, :]
        hg = jnp.dot(xb, wg_ref[...], preferred_element_type=F32)
        hu = jnp.dot(xb, wu_ref[...], preferred_element_type=F32)
        act = (hg * _sigmoid(hg) * hu).astype(BF16)
        return jnp.dot(act, wd_ref[...], preferred_element_type=F32)

    @pl.when(j == 0)
    def _():
        o_ref[...] = jnp.zeros(o_ref.shape, o_ref.dtype)

    @pl.when(j < N_GATE)
    def _():
        cols = pl.ds(pl.multiple_of(j * TF_FFN, TF_FFN), TF_FFN)
        for rows in row_chunks:
            o_ref[rows, :] += swiglu_down(rows)
            gate = _sigmoid(jnp.dot(xb_ref[rows, :], wpg_ref[...], preferred_element_type=F32))
            emb = jnp.dot(p_ref[rows, :].astype(BF16), wpe_ref[...], preferred_element_type=F32)
            o_ref[rows, cols] += gate * emb
        res = pl.ds(pl.multiple_of(j * res_rows, res_rows), res_rows)
        o_ref[res, :] += ALPHA * x_ref[...]

    @pl.when((j >= N_GATE) & (j != last))
    def _():
        for rows in row_chunks:
            o_ref[rows, :] += swiglu_down(rows)

    @pl.when(j == last)
    def _():
        mc_ln = min(MC_LN, tm)
        for m in range(0, tm, mc_ln):
            rows = pl.ds(m, mc_ln)
            o_ref[rows, :] = _layer_norm(o_ref[rows, :] + swiglu_down(rows), lg_ref[...], lb_ref[...])


def _ffn(x1, x1b, p_all, p_row0, lw, *, tm):
    n = x1.shape[0]
    assert n % tm == 0 and p_row0 % tm == 0 and tm % min(MC_FFN, tm) == 0 and tm % N_GATE == 0
    nj = D_FF // TF_FFN
    assert N_GATE < nj - 1
    gate_step = lambda j: jnp.minimum(j, N_GATE - 1)
    p_off = p_row0 // tm
    row = lambda i, j: (i, 0)
    return pl.pallas_call(
        _ffn_body,
        grid=(n // tm, nj),
        in_specs=[
            pl.BlockSpec((tm // N_GATE, D_MODEL), lambda i, j: (i * N_GATE + gate_step(j), 0)),
            pl.BlockSpec((tm, D_MODEL), row),
            pl.BlockSpec((tm, PLE_DIM), lambda i, j: (i + p_off, 0)),
            pl.BlockSpec((D_MODEL, TF_FFN), lambda i, j: (0, j)),
            pl.BlockSpec((D_MODEL, TF_FFN), lambda i, j: (0, j + nj)),
            pl.BlockSpec((TF_FFN, D_MODEL), lambda i, j: (j, 0)),
            pl.BlockSpec((D_MODEL, TF_FFN), lambda i, j: (0, gate_step(j))),
            pl.BlockSpec((PLE_DIM, TF_FFN), lambda i, j: (0, gate_step(j))),
            _const_spec((1, D_MODEL)),
            _const_spec((1, D_MODEL)),
        ],
        out_specs=pl.BlockSpec((tm, D_MODEL), row),
        out_shape=jax.ShapeDtypeStruct((n, D_MODEL), F32),
        compiler_params=pltpu.CompilerParams(
            dimension_semantics=("arbitrary", "arbitrary"), vmem_limit_bytes=VMEM_LIMIT),
        name="ffn",
    )(x1, x1b, p_all, lw["w_gu"], lw["w_gu"], lw["w_down"], lw["w_pe_gate"], lw["w_pe"],
      lw["ln2_g"], lw["ln2_b"])


def _cast_body(w_ref, o_ref):
    o_ref[...] = w_ref[...].astype(o_ref.dtype)


def _cast_bf16(w, layer, block, src_col_block=None):
    _, rows, cols = w.shape
    br, bc = block
    assert rows % br == 0 and cols % bc == 0
    src = src_col_block if src_col_block is not None else (lambda j: j)
    return pl.pallas_call(
        _cast_body,
        grid=(rows // br, cols // bc),
        in_specs=[pl.BlockSpec((None, br, bc), lambda i, j: (layer, i, src(j)))],
        out_specs=pl.BlockSpec((br, bc), lambda i, j: (i, j)),
        out_shape=jax.ShapeDtypeStruct((rows, cols), BF16),
        compiler_params=pltpu.CompilerParams(
            dimension_semantics=("arbitrary", "arbitrary"), vmem_limit_bytes=VMEM_LIMIT),
        name="cast_bf16",
    )(w)


def _w_in_src_block(j):
    n_uv = 2 * SGU_DIM // POOL_DIM
    return jnp.where(j < n_uv, j + 1, jnp.where(j == n_uv, 0, j))


def _sgu_block_diag(sgu_w, sgu_b, chunk, rows):
    w = jnp.tril(sgu_w[:, :chunk, :chunk])
    reps = rows // chunk
    eye = jnp.eye(reps, dtype=w.dtype)
    bd = jnp.einsum("ab,gts->gatbs", eye, w).reshape(SGU_GROUPS, rows, rows)
    bias = jnp.tile(jnp.transpose(sgu_b[:, :chunk]), (reps, 1))
    bias = jnp.repeat(bias, SGU_GROUP_DIM, axis=1)
    return bd.astype(BF16), bias.astype(F32)


def _layer_weights(i, w_in, pool_w, pool_scale, sgu_ln_g, sgu_ln_b, sgu_w, sgu_b, conv_w, w_br_a,
                   w_br_b, w_br_c, w_o, ln1_g, ln1_b, w_gu, w_down, w_pe, w_pe_gate, ln2_g, ln2_b,
                   sample_len):
    wide = (CAST_ROWS, D_MODEL)
    row = lambda a: a[i].reshape(1, -1).astype(F32)
    return dict(
        w_in=_cast_bf16(w_in, i, (D_MODEL, POOL_DIM), _w_in_src_block),
        sgu_ln_g=row(sgu_ln_g), sgu_ln_b=row(sgu_ln_b),
        pool_w=pool_w[i].astype(BF16), pool_scale=row(pool_scale),
        sgu_prompt=_sgu_block_diag(sgu_w[i], sgu_b[i], SGU_CHUNK, TM_MIX),
        sgu_sample=_sgu_block_diag(sgu_w[i], sgu_b[i], sample_len, TM_MIX),
        conv_w=conv_w[i].astype(F32),
        w_br_a=_cast_bf16(w_br_a, i, wide), w_br_b=_cast_bf16(w_br_b, i, wide),
        w_br_c=_cast_bf16(w_br_c, i, wide), w_o=_cast_bf16(w_o, i, wide),
        ln1_g=row(ln1_g), ln1_b=row(ln1_b),
        w_gu=_cast_bf16(w_gu, i, (D_MODEL, TF_FFN)), w_down=_cast_bf16(w_down, i, wide),
        w_pe=_cast_bf16(w_pe, i, (PLE_DIM, D_MODEL)), w_pe_gate=_cast_bf16(w_pe_gate, i, wide),
        ln2_g=row(ln2_g), ln2_b=row(ln2_b),
    )


def kernel(x_prompt, x_sample, state_pool, state_conv, p_prompt, p_sample, w_in, pool_w, pool_scale,
           sgu_ln_g, sgu_ln_b, sgu_w, sgu_b, conv_w, w_br_a, w_br_b, w_br_c, w_o, ln1_g, ln1_b, w_gu,
           w_down, w_pe, w_pe_gate, ln2_g, ln2_b):
    batch, seq, _ = x_prompt.shape
    dec_batch, dec_seq, _ = x_sample.shape
    n_p = batch * seq
    n_s = dec_batch * dec_seq
    tm_in_p = 1024
    assert seq % TM_MIX == 0 and TM_MIX % dec_seq == 0 and n_s % TM_MIX == 0
    assert n_p % tm_in_p == 0 and n_p % TM_FFN == 0

    xp = x_prompt.reshape(n_p, D_MODEL)
    xs = x_sample.reshape(n_s, D_MODEL)
    pp_all = p_prompt.reshape(DEPTH * n_p, PLE_DIM)
    ps_all = p_sample.reshape(DEPTH * n_s, PLE_DIM)
    hist_pool = jnp.pad(state_pool, ((0, 0), (0, 0), (POOL_PAD - POOL_HIST, 0), (0, 0)))
    hist_conv = jnp.pad(state_conv, ((0, 0), (0, 0), (CONV_PAD - (CONV_WIDTH - 1), 0), (0, 0)))

    outs = {k: [] for k in ("pool_p", "conv_p", "pool_s", "conv_s", "v_s")}
    for i in range(DEPTH):
        lw = _layer_weights(i, w_in, pool_w, pool_scale, sgu_ln_g, sgu_ln_b, sgu_w, sgu_b, conv_w,
                            w_br_a, w_br_b, w_br_c, w_o, ln1_g, ln1_b, w_gu, w_down, w_pe,
                            w_pe_gate, ln2_g, ln2_b, dec_seq)
        acts = _inproj(xp, lw["w_in"], lw["sgu_ln_g"], lw["sgu_ln_b"], tm=tm_in_p, v_dtype=BF16)
        x1, x1b, pool_p, conv_p = _mixer(
            xp, acts, lw, *lw["sgu_prompt"], nseq=1, seq_len=TM_MIX, pos0=0,
            tiles_per_seq=seq // TM_MIX)
        xp = _ffn(x1, x1b, pp_all, i * n_p, lw, tm=TM_FFN)
        acts = _inproj(xs, lw["w_in"], lw["sgu_ln_g"], lw["sgu_ln_b"], tm=n_s, v_dtype=F32)
        x1, x1b, pool_s, conv_s = _mixer(
            xs, acts, lw, *lw["sgu_sample"], nseq=TM_MIX // dec_seq, seq_len=dec_seq, pos0=1024,
            tiles_per_seq=1, hist=(hist_pool[i], hist_conv[i]))
        xs = _ffn(x1, x1b, ps_all, i * n_s, lw, tm=n_s)

        outs["pool_p"].append(pool_p[:, POOL_PAD - POOL_HIST:])
        outs["conv_p"].append(conv_p[:, CONV_PAD - (CONV_WIDTH - 1):])
        outs["pool_s"].append(pool_s[:, POOL_PAD - POOL_HIST:])
        outs["conv_s"].append(conv_s[:, CONV_PAD - (CONV_WIDTH - 1):])
        outs["v_s"].append(acts[3].reshape(dec_batch, dec_seq, SGU_DIM))

    return (xp.reshape(batch, seq, D_MODEL), xs.reshape(dec_batch, dec_seq, D_MODEL),
            jnp.stack(outs["pool_p"]), jnp.stack(outs["conv_p"]), jnp.stack(outs["pool_s"]),
            jnp.stack(outs["conv_s"]), jnp.stack(outs["v_s"]))
```

```python
import functools

import jax
import jax.numpy as jnp
from jax import lax
from jax.experimental import pallas as pl
from jax.experimental.pallas import tpu as pltpu

F32 = jnp.float32
BF16 = jnp.bfloat16

D_MODEL = 2048
DEPTH = 2
POOL_DIM = 512
POOL_WINDOWS = (2, 4, 8, 16)
POOL_GROUP_DIM = 128
POOL_HIST = 15
POOL_PAD = 16
SGU_DIM = 1024
SGU_GROUPS = 8
SGU_GROUP_DIM = 128
SGU_CHUNK = 128
CONV_DIM = 512
CONV_WIDTH = 3
CONV_PAD = 8
D_FF = 5632
PLE_DIM = 256
ALPHA = (2 * DEPTH) ** 0.25
PAST_LEN = 1024
LN_EPS = 1e-5
IN_COLS = 10240
GATE_COLS = 3 * D_MODEL

TM_IN = 1024
TN_IN = 1024
NC_IN = 256
MC_IN = 512
MC_LN = 256
TM_MIX = 256
TM_FFN = 1024
MC_FFN = 512
TF_FFN = 512
N_GATE = D_MODEL // TF_FFN
CAST_ROWS = 512
VMEM_LIMIT = 56 * 1024 * 1024


def _layer_norm(x, g, b):
    mu = jnp.mean(x, axis=-1, keepdims=True)
    xc = x - mu
    var = jnp.mean(xc * xc, axis=-1, keepdims=True)
    return xc * lax.rsqrt(var + LN_EPS) * g + b


def _gelu(x):
    return 0.5 * x * (1.0 + lax.erf(x * (2.0 ** -0.5)))


def _sigmoid(x):
    return 0.5 * jnp.tanh(0.5 * x) + 0.5


def _const_spec(shape):
    nd = len(shape)
    return pl.BlockSpec(shape, lambda *_: (0,) * nd, pipeline_mode=pl.Buffered(1))


def _inproj_body(x_ref, w_ref, lng_ref, lnb_ref,
                 a_ref, cb_ref, conv_ref, v_ref, u_ref, g_ref, xb_ref):
    j = pl.program_id(1)
    tm = x_ref.shape[0]
    mc = min(MC_IN, tm)
    nchunks = TN_IN // NC_IN
    half = nchunks // 2
    row_chunks = [pl.ds(m, mc) for m in range(0, tm, mc)]
    ln_chunks = [pl.ds(m, min(MC_LN, tm)) for m in range(0, tm, min(MC_LN, tm))]

    def cols(n):
        return slice(n * NC_IN, (n + 1) * NC_IN)

    def z(rows, n):
        return jnp.dot(xb_ref[rows, :], w_ref[:, cols(n)], preferred_element_type=F32)

    @pl.when(j == 0)
    def _():
        for rows in row_chunks:
            xb_ref[rows, :] = x_ref[rows, :].astype(BF16)
            for n in range(nchunks):
                u_ref[rows, cols(n)] = _gelu(z(rows, n)).astype(u_ref.dtype)

    @pl.when(j == 1)
    def _():
        for rows in ln_chunks:
            v = jnp.concatenate([_gelu(z(rows, n)) for n in range(nchunks)], axis=-1)
            v_ref[rows, :] = _layer_norm(v, lng_ref[...], lnb_ref[...]).astype(v_ref.dtype)

    @pl.when(j == 2)
    def _():
        for rows in row_chunks:
            for n in range(half):
                a_ref[rows, cols(n)] = z(rows, n)
            for n in range(half):
                cb_ref[rows, cols(n)] = z(rows, half + n).astype(cb_ref.dtype)

    @pl.when(j == 3)
    def _():
        for rows in row_chunks:
            for n in range(half):
                conv_ref[rows, cols(n)] = z(rows, n) * z(rows, half + n)

    @pl.when(j >= 4)
    def _():
        for rows in row_chunks:
            for n in range(nchunks):
                g_ref[rows, cols(n)] = _sigmoid(z(rows, n)).astype(g_ref.dtype)


def _inproj(x, w_in_p, ln_g, ln_b, *, tm, v_dtype):
    n = x.shape[0]
    assert n % tm == 0
    nj = IN_COLS // TN_IN

    row = lambda i, j: (i, 0)
    out_shape = (
        jax.ShapeDtypeStruct((n, POOL_DIM), F32),
        jax.ShapeDtypeStruct((n, CONV_DIM), BF16),
        jax.ShapeDtypeStruct((n, CONV_DIM), F32),
        jax.ShapeDtypeStruct((n, SGU_DIM), v_dtype),
        jax.ShapeDtypeStruct((n, SGU_DIM), BF16),
        jax.ShapeDtypeStruct((n, GATE_COLS), BF16),
    )
    return pl.pallas_call(
        _inproj_body,
        grid=(n // tm, nj),
        in_specs=[
            pl.BlockSpec((tm, D_MODEL), row),
            pl.BlockSpec((D_MODEL, TN_IN), lambda i, j: (0, j)),
            _const_spec((1, SGU_DIM)),
            _const_spec((1, SGU_DIM)),
        ],
        out_specs=(
            pl.BlockSpec((tm, POOL_DIM), row),
            pl.BlockSpec((tm, CONV_DIM), row),
            pl.BlockSpec((tm, CONV_DIM), row),
            pl.BlockSpec((tm, SGU_DIM), row),
            pl.BlockSpec((tm, SGU_DIM), row),
            pl.BlockSpec((tm, TN_IN), lambda i, j: (i, jnp.maximum(j - 4, 0))),
        ),
        out_shape=out_shape,
        scratch_shapes=[pltpu.VMEM((tm, D_MODEL), BF16)],
        compiler_params=pltpu.CompilerParams(
            dimension_semantics=("arbitrary", "arbitrary"), vmem_limit_bytes=VMEM_LIMIT),
        name="inproj",
    )(x, w_in_p, ln_g, ln_b)


def _mixer_body(nseq, seq_len, pos0, tiles_per_seq, has_hist, *refs):
    if has_hist:
        (x_ref, a_ref, cb_ref, cv_ref, v_ref, u_ref, g_ref, hp_ref, hc_ref,
         pw_ref, ps_ref, bd_ref, sb_ref, cw_ref, wa_ref, wb_ref, wc_ref, wo_ref, lg_ref, lb_ref,
         x1_ref, x1b_ref, np_ref, nc_ref, p_scr, q_scr) = refs
    else:
        (x_ref, a_ref, cb_ref, cv_ref, v_ref, u_ref, g_ref,
         pw_ref, ps_ref, bd_ref, sb_ref, cw_ref, wa_ref, wb_ref, wc_ref, wo_ref, lg_ref, lb_ref,
         x1_ref, x1b_ref, np_ref, nc_ref, p_scr, q_scr) = refs
    L = seq_len
    R = nseq * L
    t = pl.program_id(0) % tiles_per_seq

    if has_hist:
        p_scr[:, 0:POOL_PAD, :] = hp_ref[...]
        q_scr[:, 0:CONV_PAD, :] = hc_ref[...]
    else:
        @pl.when(t == 0)
        def _():
            p_scr[:, 0:POOL_PAD, :] = jnp.zeros((nseq, POOL_PAD, POOL_DIM), F32)
            q_scr[:, 0:CONV_PAD, :] = jnp.zeros((nseq, CONV_PAD, CONV_DIM), F32)

        @pl.when(t != 0)
        def _():
            p_scr[:, 0:POOL_PAD, :] = p_scr[:, L:L + POOL_PAD, :]
            q_scr[:, 0:CONV_PAD, :] = q_scr[:, L:L + CONV_PAD, :]
    p_scr[:, POOL_PAD:, :] = a_ref[...].reshape(nseq, L, POOL_DIM)
    q_scr[:, CONV_PAD:, :] = cv_ref[...].reshape(nseq, L, CONV_DIM)

    pos = pos0 + t * L + lax.broadcasted_iota(jnp.int32, (1, L, 1), 1) + 1
    ya = []
    for g, win in enumerate(POOL_WINDOWS):
        cols = slice(g * POOL_GROUP_DIM, (g + 1) * POOL_GROUP_DIM)
        cur = p_scr[:, POOL_PAD:POOL_PAD + L, cols]
        s = cur
        for k in range(1, win):
            s = s + p_scr[:, POOL_PAD - k:POOL_PAD - k + L, cols]
        inv_cnt = 1.0 / jnp.minimum(pos, win).astype(F32)
        pooled = (s * inv_cnt - cur).reshape(R, POOL_GROUP_DIM).astype(BF16)
        ya.append(jnp.dot(pooled, pw_ref[g], preferred_element_type=F32) * ps_ref[:, cols])
    ya = jnp.concatenate(ya, axis=-1).astype(BF16)

    vb = v_ref[...].astype(BF16)
    yb = []
    for g in range(SGU_GROUPS):
        cols = slice(g * SGU_GROUP_DIM, (g + 1) * SGU_GROUP_DIM)
        s = jnp.dot(bd_ref[g], vb[:, cols], preferred_element_type=F32) + sb_ref[:, cols]
        yb.append(u_ref[:, cols].astype(F32) * s)
    yb = jnp.concatenate(yb, axis=-1).astype(BF16)

    conv = q_scr[:, CONV_PAD - 2:CONV_PAD - 2 + L, :] * cw_ref[0:1, :]
    for k in range(1, CONV_WIDTH):
        o = CONV_PAD - 2 + k
        conv = conv + q_scr[:, o:o + L, :] * cw_ref[k:k + 1, :]
    yc = (cb_ref[...].astype(F32) * conv.reshape(R, CONV_DIM)).astype(BF16)

    ga = g_ref[:, 0:D_MODEL].astype(F32)
    gb = g_ref[:, D_MODEL:2 * D_MODEL].astype(F32)
    gc = g_ref[:, 2 * D_MODEL:].astype(F32)
    merged = ga * jnp.dot(ya, wa_ref[...], preferred_element_type=F32)
    merged = merged + gb * jnp.dot(yb, wb_ref[...], preferred_element_type=F32)
    merged = merged + gc * jnp.dot(yc, wc_ref[...], preferred_element_type=F32)
    o = jnp.dot(merged.astype(BF16), wo_ref[...], preferred_element_type=F32)
    x1 = _layer_norm(ALPHA * x_ref[...] + o, lg_ref[...], lb_ref[...])
    x1_ref[...] = x1
    x1b_ref[...] = x1.astype(BF16)

    if has_hist:
        np_ref[...] = p_scr[:, L:L + POOL_PAD, :]
        nc_ref[...] = q_scr[:, L:L + CONV_PAD, :]
    else:
        @pl.when(t == tiles_per_seq - 1)
        def _():
            np_ref[...] = p_scr[:, L:L + POOL_PAD, :]
            nc_ref[...] = q_scr[:, L:L + CONV_PAD, :]


def _mixer(x, acts, lw, sgu_bd, sgu_bias, *, nseq, seq_len, pos0, tiles_per_seq, hist=None):
    a, cb, cv, v, u, gates = acts
    n = x.shape[0]
    R = nseq * seq_len
    assert R == TM_MIX and n % R == 0
    n_tiles = n // R
    row = lambda i: (i, 0)
    has_hist = hist is not None
    n_state_seq = n_tiles * nseq // tiles_per_seq
    state_idx = lambda i: (i // tiles_per_seq, 0, 0)

    in_specs = [
        pl.BlockSpec((R, D_MODEL), row),
        pl.BlockSpec((R, POOL_DIM), row),
        pl.BlockSpec((R, CONV_DIM), row),
        pl.BlockSpec((R, CONV_DIM), row),
        pl.BlockSpec((R, SGU_DIM), row),
        pl.BlockSpec((R, SGU_DIM), row),
        pl.BlockSpec((R, GATE_COLS), row),
    ]
    args = [x, a, cb, cv, v, u, gates]
    if has_hist:
        hp, hc = hist
        in_specs += [pl.BlockSpec((nseq, POOL_PAD, POOL_DIM), state_idx),
                     pl.BlockSpec((nseq, CONV_PAD, CONV_DIM), state_idx)]
        args += [hp, hc]
    weights = [lw["pool_w"], lw["pool_scale"], sgu_bd, sgu_bias, lw["conv_w"],
               lw["w_br_a"], lw["w_br_b"], lw["w_br_c"], lw["w_o"], lw["ln1_g"], lw["ln1_b"]]
    in_specs += [_const_spec(w.shape) for w in weights]
    args += weights

    out_shape = (
        jax.ShapeDtypeStruct((n, D_MODEL), F32),
        jax.ShapeDtypeStruct((n, D_MODEL), BF16),
        jax.ShapeDtypeStruct((n_state_seq, POOL_PAD, POOL_DIM), F32),
        jax.ShapeDtypeStruct((n_state_seq, CONV_PAD, CONV_DIM), F32),
    )
    out_specs = (
        pl.BlockSpec((R, D_MODEL), row),
        pl.BlockSpec((R, D_MODEL), row),
        pl.BlockSpec((nseq, POOL_PAD, POOL_DIM), state_idx),
        pl.BlockSpec((nseq, CONV_PAD, CONV_DIM), state_idx),
    )
    return pl.pallas_call(
        functools.partial(_mixer_body, nseq, seq_len, pos0, tiles_per_seq, has_hist),
        grid=(n_tiles,),
        in_specs=in_specs,
        out_specs=out_specs,
        out_shape=out_shape,
        scratch_shapes=[pltpu.VMEM((nseq, POOL_PAD + seq_len, POOL_DIM), F32),
                        pltpu.VMEM((nseq, CONV_PAD + seq_len, CONV_DIM), F32)],
        compiler_params=pltpu.CompilerParams(
            dimension_semantics=("arbitrary",), vmem_limit_bytes=VMEM_LIMIT),
        name="mixer_sample" if has_hist else "mixer_prompt",
    )(*args)


def _ffn_body(x_ref, xb_ref, p_ref, wg_ref, wu_ref, wd_ref, wpg_ref, wpe_ref, lg_ref, lb_ref, o_ref):
    j = pl.program_id(1)
    last = pl.num_programs(1) - 1
    tm = xb_ref.shape[0]
    mc = min(MC_FFN, tm)
    row_chunks = [pl.ds(m, mc) for m in range(0, tm, mc)]
    res_rows = x_ref.shape[0]

    def swiglu_down(rows):
        xb = xb_ref[rows, :]
        hg = jnp.dot(xb, wg_ref[...], preferred_element_type=F32)
        hu = jnp.dot(xb, wu_ref[...], preferred_element_type=F32)
        act = (hg * _sigmoid(hg) * hu).astype(BF16)
        return jnp.dot(act, wd_ref[...], preferred_element_type=F32)

    @pl.when(j == 0)
    def _():
        o_ref[...] = jnp.zeros(o_ref.shape, o_ref.dtype)

    @pl.when(j < N_GATE)
    def _():
        cols = pl.ds(pl.multiple_of(j * TF_FFN, TF_FFN), TF_FFN)
        for rows in row_chunks:
            o_ref[rows, :] += swiglu_down(rows)
            gate = _sigmoid(jnp.dot(xb_ref[rows, :], wpg_ref[...], preferred_element_type=F32))
            emb = jnp.dot(p_ref[rows, :].astype(BF16), wpe_ref[...], preferred_element_type=F32)
            o_ref[rows, cols] += gate * emb
        res = pl.ds(pl.multiple_of(j * res_rows, res_rows), res_rows)
        o_ref[res, :] += ALPHA * x_ref[...]

    @pl.when((j >= N_GATE) & (j != last))
    def _():
        for rows in row_chunks:
            o_ref[rows, :] += swiglu_down(rows)

    @pl.when(j == last)
    def _():
        mc_ln = min(MC_LN, tm)
        for m in range(0, tm, mc_ln):
            rows = pl.ds(m, mc_ln)
            o_ref[rows, :] = _layer_norm(o_ref[rows, :] + swiglu_down(rows), lg_ref[...], lb_ref[...])


def _ffn(x1, x1b, p_all, p_row0, lw, *, tm):
    n = x1.shape[0]
    assert n % tm == 0 and p_row0 % tm == 0 and tm % min(MC_FFN, tm) == 0 and tm % N_GATE == 0
    nj = D_FF // TF_FFN
    assert N_GATE < nj - 1
    gate_step = lambda j: jnp.minimum(j, N_GATE - 1)
    p_off = p_row0 // tm
    row = lambda i, j: (i, 0)
    return pl.pallas_call(
        _ffn_body,
        grid=(n // tm, nj),
        in_specs=[
            pl.BlockSpec((tm // N_GATE, D_MODEL), lambda i, j: (i * N_GATE + gate_step(j), 0)),
            pl.BlockSpec((tm, D_MODEL), row),
            pl.BlockSpec((tm, PLE_DIM), lambda i, j: (i + p_off, 0)),
            pl.BlockSpec((D_MODEL, TF_FFN), lambda i, j: (0, j)),
            pl.BlockSpec((D_MODEL, TF_FFN), lambda i, j: (0, j + nj)),
            pl.BlockSpec((TF_FFN, D_MODEL), lambda i, j: (j, 0)),
            pl.BlockSpec((D_MODEL, TF_FFN), lambda i, j: (0, gate_step(j))),
            pl.BlockSpec((PLE_DIM, TF_FFN), lambda i, j: (0, gate_step(j))),
            _const_spec((1, D_MODEL)),
            _const_spec((1, D_MODEL)),
        ],
        out_specs=pl.BlockSpec((tm, D_MODEL), row),
        out_shape=jax.ShapeDtypeStruct((n, D_MODEL), F32),
        compiler_params=pltpu.CompilerParams(
            dimension_semantics=("arbitrary", "arbitrary"), vmem_limit_bytes=VMEM_LIMIT),
        name="ffn",
    )(x1, x1b, p_all, lw["w_gu"], lw["w_gu"], lw["w_down"], lw["w_pe_gate"], lw["w_pe"],
      lw["ln2_g"], lw["ln2_b"])


def _cast_body(w_ref, o_ref):
    o_ref[...] = w_ref[...].astype(o_ref.dtype)


def _cast_bf16(w, layer, block, src_col_block=None):
    _, rows, cols = w.shape
    br, bc = block
    assert rows % br == 0 and cols % bc == 0
    src = src_col_block if src_col_block is not None else (lambda j: j)
    return pl.pallas_call(
        _cast_body,
        grid=(rows // br, cols // bc),
        in_specs=[pl.BlockSpec((None, br, bc), lambda i, j: (layer, i, src(j)))],
        out_specs=pl.BlockSpec((br, bc), lambda i, j: (i, j)),
        out_shape=jax.ShapeDtypeStruct((rows, cols), BF16),
        compiler_params=pltpu.CompilerParams(
            dimension_semantics=("arbitrary", "arbitrary"), vmem_limit_bytes=VMEM_LIMIT),
        name="cast_bf16",
    )(w)


def _w_in_src_block(j):
    n_uv = 2 * SGU_DIM // POOL_DIM
    return jnp.where(j < n_uv, j + 1, jnp.where(j == n_uv, 0, j))


def _sgu_block_diag(sgu_w, sgu_b, chunk, rows):
    w = jnp.tril(sgu_w[:, :chunk, :chunk])
    reps = rows // chunk
    eye = jnp.eye(reps, dtype=w.dtype)
    bd = jnp.einsum("ab,gts->gatbs", eye, w).reshape(SGU_GROUPS, rows, rows)
    bias = jnp.tile(jnp.transpose(sgu_b[:, :chunk]), (reps, 1))
    bias = jnp.repeat(bias, SGU_GROUP_DIM, axis=1)
    return bd.astype(BF16), bias.astype(F32)


def _layer_weights(i, w_in, pool_w, pool_scale, sgu_ln_g, sgu_ln_b, sgu_w, sgu_b, conv_w, w_br_a,
                   w_br_b, w_br_c, w_o, ln1_g, ln1_b, w_gu, w_down, w_pe, w_pe_gate, ln2_g, ln2_b,
                   sample_len):
    wide = (CAST_ROWS, D_MODEL)
    row = lambda a: a[i].reshape(1, -1).astype(F32)
    return dict(
        w_in=_cast_bf16(w_in, i, (D_MODEL, POOL_DIM), _w_in_src_block),
        sgu_ln_g=row(sgu_ln_g), sgu_ln_b=row(sgu_ln_b),
        pool_w=pool_w[i].astype(BF16), pool_scale=row(pool_scale),
        sgu_prompt=_sgu_block_diag(sgu_w[i], sgu_b[i], SGU_CHUNK, TM_MIX),
        sgu_sample=_sgu_block_diag(sgu_w[i], sgu_b[i], sample_len, TM_MIX),
        conv_w=conv_w[i].astype(F32),
        w_br_a=_cast_bf16(w_br_a, i, wide), w_br_b=_cast_bf16(w_br_b, i, wide),
        w_br_c=_cast_bf16(w_br_c, i, wide), w_o=_cast_bf16(w_o, i, wide),
        ln1_g=row(ln1_g), ln1_b=row(ln1_b),
        w_gu=_cast_bf16(w_gu, i, (D_MODEL, TF_FFN)), w_down=_cast_bf16(w_down, i, wide),
        w_pe=_cast_bf16(w_pe, i, (PLE_DIM, D_MODEL)), w_pe_gate=_cast_bf16(w_pe_gate, i, wide),
        ln2_g=row(ln2_g), ln2_b=row(ln2_b),
    )


def kernel(x_prompt, x_sample, state_pool, state_conv, p_prompt, p_sample, w_in, pool_w, pool_scale,
           sgu_ln_g, sgu_ln_b, sgu_w, sgu_b, conv_w, w_br_a, w_br_b, w_br_c, w_o, ln1_g, ln1_b, w_gu,
           w_down, w_pe, w_pe_gate, ln2_g, ln2_b):
    batch, seq, _ = x_prompt.shape
    dec_batch, dec_seq, _ = x_sample.shape
    n_p = batch * seq
    n_s = dec_batch * dec_seq
    assert seq % TM_MIX == 0 and TM_MIX % dec_seq == 0 and n_s % TM_MIX == 0
    assert n_p % TM_IN == 0 and n_p % TM_FFN == 0

    xp = x_prompt.reshape(n_p, D_MODEL)
    xs = x_sample.reshape(n_s, D_MODEL)
    pp_all = p_prompt.reshape(DEPTH * n_p, PLE_DIM)
    ps_all = p_sample.reshape(DEPTH * n_s, PLE_DIM)
    hist_pool = jnp.pad(state_pool, ((0, 0), (0, 0), (POOL_PAD - POOL_HIST, 0), (0, 0)))
    hist_conv = jnp.pad(state_conv, ((0, 0), (0, 0), (CONV_PAD - (CONV_WIDTH - 1), 0), (0, 0)))

    outs = {k: [] for k in ("pool_p", "conv_p", "pool_s", "conv_s", "v_s")}
    for i in range(DEPTH):
        lw = _layer_weights(i, w_in, pool_w, pool_scale, sgu_ln_g, sgu_ln_b, sgu_w, sgu_b, conv_w,
                            w_br_a, w_br_b, w_br_c, w_o, ln1_g, ln1_b, w_gu, w_down, w_pe,
                            w_pe_gate, ln2_g, ln2_b, dec_seq)
        acts = _inproj(xp, lw["w_in"], lw["sgu_ln_g"], lw["sgu_ln_b"], tm=TM_IN, v_dtype=BF16)
        x1, x1b, pool_p, conv_p = _mixer(
            xp, acts, lw, *lw["sgu_prompt"], nseq=1, seq_len=TM_MIX, pos0=0,
            tiles_per_seq=seq // TM_MIX)
        xp = _ffn(x1, x1b, pp_all, i * n_p, lw, tm=TM_FFN)
        acts = _inproj(xs, lw["w_in"], lw["sgu_ln_g"], lw["sgu_ln_b"], tm=n_s, v_dtype=F32)
        x1, x1b, pool_s, conv_s = _mixer(
            xs, acts, lw, *lw["sgu_sample"], nseq=TM_MIX // dec_seq, seq_len=dec_seq, pos0=PAST_LEN,
            tiles_per_seq=1, hist=(hist_pool[i], hist_conv[i]))
        xs = _ffn(x1, x1b, ps_all, i * n_s, lw, tm=n_s)

        outs["pool_p"].append(pool_p[:, POOL_PAD - POOL_HIST:])
        outs["conv_p"].append(conv_p[:, CONV_PAD - (CONV_WIDTH - 1):])
        outs["pool_s"].append(pool_s[:, POOL_PAD - POOL_HIST:])
        outs["conv_s"].append(conv_s[:, CONV_PAD - (CONV_WIDTH - 1):])
        outs["v_s"].append(acts[3].reshape(dec_batch, dec_seq, SGU_DIM))

    return (xp.reshape(batch, seq, D_MODEL), xs.reshape(dec_batch, dec_seq, D_MODEL),
            jnp.stack(outs["pool_p"]), jnp.stack(outs["conv_p"]), jnp.stack(outs["pool_s"]),
            jnp.stack(outs["conv_s"]), jnp.stack(outs["v_s"]))
```
